```python
import jax, jax.numpy as jnp
from jax import lax
import numpy as np

D_MODEL = 2048
BATCH = 1
SEQ = 8192
DEPTH = 1

N_META = 16
D_MIX = D_MODEL
HEAD_DIM = 128
ATT_WIDTH = D_MIX // 2
N_ATT_HEADS = ATT_WIDTH // HEAD_DIM
CONV_CH = D_MIX - ATT_WIDTH
N_CONV_GROUPS = 8
CONV_KERNEL = 31
Q_BLOCK = 128
IN_COLS = 3 * ATT_WIDTH + N_ATT_HEADS + 2 * CONV_CH
SPLITS = [ATT_WIDTH, 2 * ATT_WIDTH, 3 * ATT_WIDTH, 3 * ATT_WIDTH + N_ATT_HEADS,
          3 * ATT_WIDTH + N_ATT_HEADS + CONV_CH]
N_EXPERTS = 32
TOP_K = 4
D_FF = D_MODEL
SWIGLU_LIMIT = 7.0
SWIGLU_ALPHA = 1.702
MOE_BLOCK = 128
NORM_EPS = 1e-5
NEG_BIG = -1e30

kernel_name = "hymba_fox_conformer_moe_block"


def rms_norm(x, g):
    xf = x.astype(jnp.float32)
    y = xf * lax.rsqrt(jnp.mean(xf * xf, axis=-1, keepdims=True) + NORM_EPS)
    return (y * g.astype(jnp.float32)).astype(x.dtype)


def group_norm(u, g, b):
    bsz, length, ch = u.shape
    ug = u.astype(jnp.float32).reshape(bsz, length, N_CONV_GROUPS, ch // N_CONV_GROUPS)
    mu = jnp.mean(ug, axis=-1, keepdims=True)
    var = jnp.mean(jnp.square(ug - mu), axis=-1, keepdims=True)
    y = ((ug - mu) * lax.rsqrt(var + NORM_EPS)).reshape(bsz, length, ch)
    return (y * g.astype(jnp.float32) + b.astype(jnp.float32)).astype(u.dtype)


def conformer_conv_mixer(glu_a, glu_g, b_glu, conv_w, conv_b, gn_g, gn_b):
    u = (glu_a + b_glu[:CONV_CH]) * jax.nn.sigmoid(glu_g + b_glu[CONV_CH:])
    u = lax.conv_general_dilated(
        u, conv_w[:, None, :].astype(u.dtype), window_strides=(1,),
        padding=[(CONV_KERNEL - 1, 0)], dimension_numbers=("NWC", "WIO", "NWC"),
        feature_group_count=CONV_CH) + conv_b
    return jax.nn.silu(group_norm(u, gn_g, gn_b))


def forgetting_attention(q, k, v, log_f):
    bsz, length, heads, dh = q.shape
    pad = Q_BLOCK - N_META
    pw = ((0, 0), (pad, 0), (0, 0), (0, 0))
    q, k, v = jnp.pad(q, pw), jnp.pad(k, pw), jnp.pad(v, pw)
    log_f = jnp.pad(log_f, ((0, 0), (pad, 0), (0, 0)))
    lp = length + pad
    nb = lp // Q_BLOCK
    c = jnp.cumsum(log_f, axis=1)
    c_k = c.transpose(0, 2, 1)
    kh = k.transpose(0, 2, 1, 3)
    vh = v.transpose(0, 2, 1, 3)
    qb = q.reshape(bsz, nb, Q_BLOCK, heads, dh).transpose(1, 0, 3, 2, 4)
    cb = c.reshape(bsz, nb, Q_BLOCK, heads).transpose(1, 0, 3, 2)
    key_pos = jnp.arange(lp)
    scale = 1.0 / float(np.sqrt(dh))

    def block(args):
        i, qi, ci = args
        s = jnp.einsum("bhqd,bhkd->bhqk", qi, kh).astype(jnp.float32) * scale
        s = s + ci[..., None] - c_k[:, :, None, :]
        q_pos = i * Q_BLOCK + jnp.arange(Q_BLOCK)
        valid = (key_pos[None, :] <= q_pos[:, None]) & (key_pos[None, :] >= pad)
        s = jnp.where(valid, s, NEG_BIG)
        p = jax.nn.softmax(s, axis=-1)
        return jnp.einsum("bhqk,bhkd->bhqd", p.astype(vh.dtype), vh)

    o = lax.map(block, (jnp.arange(nb), qb, cb))
    o = o.transpose(1, 0, 3, 2, 4).reshape(bsz, lp, heads * dh)
    return o[:, pad:, :]


def moe_ffn(h, router_w, router_b, w_gate_up, b_gate_up, w_down, b_down):
    t, d = h.shape
    logits = (h @ router_w).astype(jnp.float32) + router_b.astype(jnp.float32)
    top_v, top_i = lax.top_k(logits, TOP_K)
    gates = jax.nn.softmax(top_v, axis=-1)
    tk = t * TOP_K
    flat_e = top_i.reshape(-1)
    flat_tok = jnp.arange(tk) // TOP_K
    order = jnp.argsort(flat_e)
    sorted_e = flat_e[order]
    sorted_tok = flat_tok[order]
    sorted_gate = gates.reshape(-1)[order]
    counts = jnp.zeros((N_EXPERTS,), jnp.int32).at[flat_e].add(1)
    grp_start = jnp.cumsum(counts) - counts
    padded = ((counts + MOE_BLOCK - 1) // MOE_BLOCK) * MOE_BLOCK
    pad_end = jnp.cumsum(padded)
    pad_start = pad_end - padded
    dest = pad_start[sorted_e] + (jnp.arange(tk) - grp_start[sorted_e])
    n_blocks = -(-(tk + N_EXPERTS * (MOE_BLOCK - 1)) // MOE_BLOCK)
    n_rows = n_blocks * MOE_BLOCK
    row_tok = jnp.full((n_rows,), t, jnp.int32).at[dest].set(sorted_tok)
    h_ext = jnp.concatenate([h, jnp.zeros((1, d), h.dtype)], axis=0)
    xb = h_ext[row_tok].reshape(n_blocks, MOE_BLOCK, d)
    block_expert = jnp.minimum(
        jnp.searchsorted(pad_end, jnp.arange(n_blocks) * MOE_BLOCK, side="right"), N_EXPERTS - 1)

    def expert_block(args):
        xblk, e = args
        gu = xblk @ w_gate_up[e] + b_gate_up[e]
        gate, up = jnp.split(gu, 2, axis=-1)
        gate = jnp.minimum(gate, SWIGLU_LIMIT)
        up = jnp.clip(up, -SWIGLU_LIMIT, SWIGLU_LIMIT)
        act = gate * jax.nn.sigmoid(SWIGLU_ALPHA * gate) * (up + 1.0)
        return act @ w_down[e] + b_down[e]

    yb = lax.map(expert_block, (xb, block_expert)).reshape(n_rows, d)
    contrib = yb[dest] * sorted_gate[:, None].astype(yb.dtype)
    return jnp.zeros((t, d), yb.dtype).at[sorted_tok].add(contrib)


def setup_inputs(seed: int = 0) -> dict:
    key = jax.random.key(seed)
    ks = jax.random.split(key, 20)
    f32 = jnp.float32
    nrm = lambda k, shape, s: (jax.random.normal(k, shape, f32) * s)
    x = jax.random.normal(ks[0], (BATCH, SEQ, D_MODEL), f32)
    meta_tokens = nrm(ks[1], (N_META, D_MODEL), 1.0)
    norm_mix_g = 1.0 + nrm(ks[2], (DEPTH, D_MODEL), 0.02)
    w_in = nrm(ks[3], (DEPTH, D_MODEL, IN_COLS), D_MODEL ** -0.5)
    b_fgate = jnp.linspace(1.0, 6.0, N_ATT_HEADS, dtype=f32)[None, :] + nrm(ks[4], (DEPTH, N_ATT_HEADS), 0.1)
    b_glu = nrm(ks[5], (DEPTH, 2 * CONV_CH), 0.02)
    conv_w = nrm(ks[6], (DEPTH, CONV_KERNEL, CONV_CH), CONV_KERNEL ** -0.5)
    conv_b = nrm(ks[7], (DEPTH, CONV_CH), 0.02)
    gn_g = 1.0 + nrm(ks[8], (DEPTH, CONV_CH), 0.02)
    gn_b = nrm(ks[9], (DEPTH, CONV_CH), 0.02)
    w_out = nrm(ks[10], (DEPTH, D_MIX, D_MODEL), D_MIX ** -0.5)
    norm_ffn_g = 1.0 + nrm(ks[11], (DEPTH, D_MODEL), 0.02)
    router_w = nrm(ks[12], (DEPTH, D_MODEL, N_EXPERTS), D_MODEL ** -0.5)
    router_b = nrm(ks[13], (DEPTH, N_EXPERTS), 0.01)
    w_gate_up = nrm(ks[14], (DEPTH, N_EXPERTS, D_MODEL, 2 * D_FF), D_MODEL ** -0.5)
    b_gate_up = nrm(ks[15], (DEPTH, N_EXPERTS, 2 * D_FF), 0.02)
    w_down = nrm(ks[16], (DEPTH, N_EXPERTS, D_FF, D_MODEL), D_FF ** -0.5)
    b_down = nrm(ks[17], (DEPTH, N_EXPERTS, D_MODEL), 0.02)
    norm_final_g = 1.0 + nrm(ks[18], (D_MODEL,), 0.02)
    return {"x": x, "meta_tokens": meta_tokens, "norm_mix_g": norm_mix_g, "w_in": w_in,
            "b_fgate": b_fgate, "b_glu": b_glu, "conv_w": conv_w, "conv_b": conv_b,
            "gn_g": gn_g, "gn_b": gn_b, "w_out": w_out, "norm_ffn_g": norm_ffn_g,
            "router_w": router_w, "router_b": router_b, "w_gate_up": w_gate_up,
            "b_gate_up": b_gate_up, "w_down": w_down, "b_down": b_down,
            "norm_final_g": norm_final_g}


def reference(x, meta_tokens, norm_mix_g, w_in, b_fgate, b_glu, conv_w, conv_b, gn_g, gn_b,
              w_out, norm_ffn_g, router_w, router_b, w_gate_up, b_gate_up, w_down, b_down,
              norm_final_g):
    bsz = x.shape[0]
    meta = jnp.broadcast_to(meta_tokens[None].astype(x.dtype), (bsz, N_META, D_MODEL))
    h = jnp.concatenate([meta, x], axis=1)
    length = h.shape[1]
    for layer in range(DEPTH):
        hn = rms_norm(h, norm_mix_g[layer])
        proj = hn @ w_in[layer]
        q, k, v, f_logit, glu_a, glu_g = jnp.split(proj, SPLITS, axis=-1)
        hs = (bsz, length, N_ATT_HEADS, HEAD_DIM)
        log_f = jax.nn.log_sigmoid((f_logit + b_fgate[layer]).astype(jnp.float32))
        att = forgetting_attention(q.reshape(hs), k.reshape(hs), v.reshape(hs), log_f)
        conv = conformer_conv_mixer(glu_a, glu_g, b_glu[layer], conv_w[layer], conv_b[layer],
                                    gn_g[layer], gn_b[layer])
        mixed = jnp.concatenate([att, conv], axis=-1)
        h = h + mixed @ w_out[layer]
        hn2 = rms_norm(h, norm_ffn_g[layer]).reshape(bsz * length, D_MODEL)
        ff = moe_ffn(hn2, router_w[layer], router_b[layer], w_gate_up[layer], b_gate_up[layer],
                     w_down[layer], b_down[layer])
        h = h + ff.reshape(bsz, length, D_MODEL)
    return rms_norm(h, norm_final_g)[:, N_META:, :]
```

```python
import functools

import jax
import jax.numpy as jnp
import numpy as np
from jax import lax
from jax.experimental import pallas as pl
from jax.experimental.pallas import tpu as pltpu

D_MODEL = 2048
N_META = 16
HEAD_DIM = 128
ATT_WIDTH = 1024
N_HEADS = ATT_WIDTH // HEAD_DIM
CONV_CH = 1024
N_GROUPS = 8
GROUP_CH = CONV_CH // N_GROUPS
CONV_K = 31
N_EXPERTS = 32
TOP_K = 4
D_FF = 2048
SWIGLU_LIMIT = 7.0
SWIGLU_ALPHA = 1.702
NORM_EPS = 1e-5
NEG_BIG = -1e30

LANES = 128
ROW_CHUNKS = D_MODEL // LANES
VMEM_LIMIT = 56 * 1024 * 1024

F32 = jnp.float32
BF16 = jnp.bfloat16


def _params(sem, vmem=VMEM_LIMIT):
    return pltpu.CompilerParams(dimension_semantics=sem, vmem_limit_bytes=vmem)


def _inproj_body(x_ref, g_ref, w_ref, wf_ref, o_ref, f_ref, hn_ref):
    @pl.when(pl.program_id(1) == 0)
    def _():
        x = x_ref[...]
        inv = lax.rsqrt(jnp.mean(x * x, axis=-1, keepdims=True) + NORM_EPS)
        hn = (x * inv * g_ref[...]).astype(BF16)
        hn_ref[...] = hn
        f_ref[...] = jnp.dot(hn, wf_ref[...], preferred_element_type=F32)

    o_ref[...] = jnp.dot(hn_ref[...], w_ref[...], preferred_element_type=F32).astype(o_ref.dtype)


def _inproj(x, g, w_main, w_f, tm, tn):
    m, d = x.shape
    n = w_main.shape[1]
    return pl.pallas_call(
        _inproj_body,
        grid=(m // tm, n // tn),
        in_specs=[
            pl.BlockSpec((tm, d), lambda i, j: (i, 0)),
            pl.BlockSpec((1, d), lambda i, j: (0, 0)),
            pl.BlockSpec((d, tn), lambda i, j: (0, j)),
            pl.BlockSpec((d, LANES), lambda i, j: (0, 0)),
        ],
        out_specs=[
            pl.BlockSpec((tm, tn), lambda i, j: (i, j)),
            pl.BlockSpec((tm, LANES), lambda i, j: (i, 0)),
        ],
        out_shape=[
            jax.ShapeDtypeStruct((m, n), BF16),
            jax.ShapeDtypeStruct((m, LANES), F32),
        ],
        scratch_shapes=[pltpu.VMEM((tm, d), BF16)],
        compiler_params=_params(("arbitrary", "arbitrary")),
        name="inproj",
    )(x, g, w_main, w_f)


def _attn_body(it_ref, jt_ref, q_ref, k_ref, v_ref, km_ref, vm_ref, cq_ref, ck_ref, ckm_ref,
               o_ref, m_ref, l_ref, acc_ref, *, scale):
    t = pl.program_id(1)
    i = it_ref[t]
    j = jt_ref[t]
    q = q_ref[...]
    cq = cq_ref[...]
    contract_last = (((1,), (1,)), ((), ()))

    @pl.when(j == 0)
    def _():
        s = lax.dot_general(q, km_ref[...], contract_last, preferred_element_type=F32)
        s = s * scale + cq - ckm_ref[...]
        m = jnp.max(s, axis=-1, keepdims=True)
        p = jnp.exp(s - m)
        m_ref[...] = m
        l_ref[...] = jnp.sum(p, axis=-1, keepdims=True)
        acc_ref[...] = jnp.dot(p.astype(BF16), vm_ref[...], preferred_element_type=F32)

    def step(masked):
        s = lax.dot_general(q, k_ref[...], contract_last, preferred_element_type=F32)
        s = s * scale + cq - ck_ref[...]
        if masked:
            row = lax.broadcasted_iota(jnp.int32, s.shape, 0)
            col = lax.broadcasted_iota(jnp.int32, s.shape, 1)
            s = jnp.where(col <= row, s, NEG_BIG)
        m_prev = m_ref[...]
        m_new = jnp.maximum(m_prev, jnp.max(s, axis=-1, keepdims=True))
        alpha = jnp.exp(m_prev - m_new)
        p = jnp.exp(s - m_new)
        l_ref[...] = alpha * l_ref[...] + jnp.sum(p, axis=-1, keepdims=True)
        acc_ref[...] = alpha * acc_ref[...] + jnp.dot(
            p.astype(BF16), v_ref[...], preferred_element_type=F32)
        m_ref[...] = m_new

    @pl.when(j < i)
    def _():
        step(False)

    @pl.when(j == i)
    def _():
        step(True)
        o_ref[...] = (acc_ref[...] / l_ref[...]).astype(o_ref.dtype)


def _attention(proj, proj_meta, cq, ck, ckm, tq):
    t_real = proj.shape[0]
    nq = t_real // tq
    it = np.concatenate([np.full(i + 1, i, np.int32) for i in range(nq)])
    jt = np.concatenate([np.arange(i + 1, dtype=np.int32) for i in range(nq)])
    n_tri = it.shape[0]
    kcol = ATT_WIDTH // HEAD_DIM
    vcol = 2 * ATT_WIDTH // HEAD_DIM
    grid_spec = pltpu.PrefetchScalarGridSpec(
        num_scalar_prefetch=2,
        grid=(N_HEADS, n_tri),
        in_specs=[
            pl.BlockSpec((tq, HEAD_DIM), lambda h, t, it, jt: (it[t], h)),
            pl.BlockSpec((tq, HEAD_DIM), lambda h, t, it, jt: (jt[t], kcol + h)),
            pl.BlockSpec((tq, HEAD_DIM), lambda h, t, it, jt: (jt[t], vcol + h)),
            pl.BlockSpec((N_META, HEAD_DIM), lambda h, t, it, jt: (0, kcol + h)),
            pl.BlockSpec((N_META, HEAD_DIM), lambda h, t, it, jt: (0, vcol + h)),
            pl.BlockSpec((None, tq, 1), lambda h, t, it, jt: (h, it[t], 0)),
            pl.BlockSpec((None, 1, tq), lambda h, t, it, jt: (h, 0, jt[t])),
            pl.BlockSpec((None, 1, N_META), lambda h, t, it, jt: (h, 0, 0)),
        ],
        out_specs=pl.BlockSpec((tq, HEAD_DIM), lambda h, t, it, jt: (it[t], h)),
        scratch_shapes=[
            pltpu.VMEM((tq, 1), F32),
            pltpu.VMEM((tq, 1), F32),
            pltpu.VMEM((tq, HEAD_DIM), F32),
        ],
    )
    return pl.pallas_call(
        functools.partial(_attn_body, scale=1.0 / float(np.sqrt(HEAD_DIM))),
        grid_spec=grid_spec,
        out_shape=jax.ShapeDtypeStruct((t_real, ATT_WIDTH), BF16),
        compiler_params=_params(("arbitrary", "arbitrary")),
        name="fox_attention",
    )(jnp.asarray(it), jnp.asarray(jt), proj, proj, proj, proj_meta, proj_meta, cq, ck, ckm)


CONV_HIST = 32
CONV_ROWS = 128


def _glu(a_ref, g_ref, bglu_ref):
    a = a_ref[...].astype(F32) + bglu_ref[:, :CONV_CH]
    g = g_ref[...].astype(F32) + bglu_ref[:, CONV_CH:]
    return a * jax.nn.sigmoid(g)


def _conv_body(a_ref, g_ref, am_ref, gm_ref, bglu_ref, w_ref, cb_ref, gng_ref, gnb_ref,
               o_ref, ubuf, *, tm):
    i = pl.program_id(0)

    @pl.when(i == 0)
    def _():
        ubuf[0:CONV_HIST - N_META, :] = jnp.zeros((CONV_HIST - N_META, CONV_CH), F32)
        ubuf[CONV_HIST - N_META:CONV_HIST, :] = _glu(am_ref, gm_ref, bglu_ref)

    @pl.when(i > 0)
    def _():
        ubuf[0:CONV_HIST, :] = ubuf[tm:tm + CONV_HIST, :]

    ubuf[CONV_HIST:CONV_HIST + tm, :] = _glu(a_ref, g_ref, bglu_ref)

    first = CONV_HIST - (CONV_K - 1)
    for grp in range(N_GROUPS):
        lanes = slice(grp * GROUP_CH, (grp + 1) * GROUP_CH)
        for rc in range(tm // CONV_ROWS):
            base = rc * CONV_ROWS
            acc = jnp.zeros((CONV_ROWS, GROUP_CH), F32)
            for tap in range(CONV_K):
                acc = acc + ubuf[base + first + tap:base + first + tap + CONV_ROWS, lanes] * \
                    w_ref[tap:tap + 1, lanes]
            acc = acc + cb_ref[:, lanes]
            mu = jnp.mean(acc, axis=-1, keepdims=True)
            dlt = acc - mu
            var = jnp.mean(dlt * dlt, axis=-1, keepdims=True)
            y = dlt * lax.rsqrt(var + NORM_EPS) * gng_ref[:, lanes] + gnb_ref[:, lanes]
            o_ref[base:base + CONV_ROWS, lanes] = (y * jax.nn.sigmoid(y)).astype(o_ref.dtype)


def _conv_mixer(proj, proj_meta, b_glu, conv_w, conv_b, gn_g, gn_b, tm):
    t_real = proj.shape[0]
    acol = 3 * ATT_WIDTH // CONV_CH
    gcol = acol + 1
    const = lambda i: (0, 0)
    return pl.pallas_call(
        functools.partial(_conv_body, tm=tm),
        grid=(t_real // tm,),
        in_specs=[
            pl.BlockSpec((tm, CONV_CH), lambda i: (i, acol)),
            pl.BlockSpec((tm, CONV_CH), lambda i: (i, gcol)),
            pl.BlockSpec((N_META, CONV_CH), lambda i: (0, acol)),
            pl.BlockSpec((N_META, CONV_CH), lambda i: (0, gcol)),
            pl.BlockSpec((1, 2 * CONV_CH), const),
            pl.BlockSpec((CONV_K, CONV_CH), const),
            pl.BlockSpec((1, CONV_CH), const),
            pl.BlockSpec((1, CONV_CH), const),
            pl.BlockSpec((1, CONV_CH), const),
        ],
        out_specs=pl.BlockSpec((tm, CONV_CH), lambda i: (i, 0)),
        out_shape=jax.ShapeDtypeStruct((t_real, CONV_CH), BF16),
        scratch_shapes=[pltpu.VMEM((tm + CONV_HIST, CONV_CH), F32)],
        compiler_params=_params(("arbitrary",)),
        name="conv_mixer",
    )(proj, proj, proj_meta, proj_meta, b_glu, conv_w, conv_b, gn_g, gn_b)


def _outproj_body(att_ref, conv_ref, x_ref, w_ref, g_ref, rw_ref, rb_ref,
                  h_ref, hnc_ref, topi_ref, gate_ref, *, tm):
    mix = jnp.dot(att_ref[...], w_ref[0:ATT_WIDTH, :], preferred_element_type=F32)
    mix = mix + jnp.dot(conv_ref[...], w_ref[ATT_WIDTH:, :], preferred_element_type=F32)
    h = x_ref[...] + mix
    h_ref[...] = h
    inv = lax.rsqrt(jnp.mean(h * h, axis=-1, keepdims=True) + NORM_EPS)
    hn = h * inv * g_ref[...]
    for s in range(ROW_CHUNKS):
        hnc_ref[pl.ds(s, tm, stride=ROW_CHUNKS), :] = hn[:, s * LANES:(s + 1) * LANES]

    logits = jnp.dot(hn, rw_ref[...], preferred_element_type=F32,
                     precision=lax.Precision.HIGHEST) + rb_ref[...]
    lane = lax.broadcasted_iota(jnp.int32, logits.shape, 1)
    vals = logits
    tops = []
    idxs = []
    for _ in range(TOP_K):
        mx = jnp.max(vals, axis=-1, keepdims=True)
        ix = jnp.min(jnp.where(vals == mx, lane, LANES), axis=-1, keepdims=True)
        tops.append(mx)
        idxs.append(ix)
        vals = jnp.where(lane == ix, -jnp.inf, vals)
    exps = [jnp.exp(v - tops[0]) for v in tops]
    denom = exps[0] + exps[1] + exps[2] + exps[3]
    topi = jnp.zeros(logits.shape, jnp.int32)
    gate = jnp.zeros(logits.shape, F32)
    for k in range(TOP_K):
        topi = jnp.where(lane == k, idxs[k], topi)
        gate = jnp.where(lane == k, exps[k] / denom, gate)
    topi_ref[...] = topi
    gate_ref[...] = gate


def _outproj_router(att, conv, x, w_out, g, router_w, router_b, tm):
    t_real, d = x.shape
    const = lambda i: (0, 0)
    return pl.pallas_call(
        functools.partial(_outproj_body, tm=tm),
        grid=(t_real // tm,),
        in_specs=[
            pl.BlockSpec((tm, ATT_WIDTH), lambda i: (i, 0)),
            pl.BlockSpec((tm, CONV_CH), lambda i: (i, 0)),
            pl.BlockSpec((tm, d), lambda i: (i, 0)),
            pl.BlockSpec((d, d), const),
            pl.BlockSpec((1, d), const),
            pl.BlockSpec((d, LANES), const),
            pl.BlockSpec((1, LANES), const),
        ],
        out_specs=[
            pl.BlockSpec((tm, d), lambda i: (i, 0)),
            pl.BlockSpec((tm * ROW_CHUNKS, LANES), lambda i: (i, 0)),
            pl.BlockSpec((tm, LANES), lambda i: (i, 0)),
            pl.BlockSpec((tm, LANES), lambda i: (i, 0)),
        ],
        out_shape=[
            jax.ShapeDtypeStruct((t_real, d), F32),
            jax.ShapeDtypeStruct((t_real * ROW_CHUNKS, LANES), F32),
            jax.ShapeDtypeStruct((t_real, LANES), jnp.int32),
            jax.ShapeDtypeStruct((t_real, LANES), F32),
        ],
        compiler_params=_params(("arbitrary",)),
        name="outproj_router",
    )(att, conv, x, w_out, g, router_w, router_b)


MOE_TM = 256


def _row_copy(src_hbm, dst, tok, slot, sem):
    return pltpu.make_async_copy(
        src_hbm.at[pl.ds(pl.multiple_of(tok * ROW_CHUNKS, ROW_CHUNKS), ROW_CHUNKS), :],
        dst.at[pl.ds(pl.multiple_of(slot * ROW_CHUNKS, ROW_CHUNKS), ROW_CHUNKS), :],
        sem)


def _gather_body(nused_ref, tok_ref, src_hbm, o_ref, stage, sem):
    t = pl.program_id(0)

    @pl.when(t < nused_ref[0])
    def _():
        def issue(r, carry):
            _row_copy(src_hbm, stage, tok_ref[0, r], r, sem).start()
            return carry

        lax.fori_loop(0, MOE_TM, issue, 0)

        def drain(r, carry):
            _row_copy(src_hbm, stage, tok_ref[0, r], r, sem).wait()
            return carry

        lax.fori_loop(0, MOE_TM, drain, 0)
        for s in range(ROW_CHUNKS):
            o_ref[:, s * LANES:(s + 1) * LANES] = \
                stage[pl.ds(s, MOE_TM, stride=ROW_CHUNKS), :].astype(o_ref.dtype)

    @pl.when(t >= nused_ref[0])
    def _():
        o_ref[...] = jnp.zeros(o_ref.shape, o_ref.dtype)


def _gather_rows(n_used, row_tok, hn_chunks, n_tiles):
    grid_spec = pltpu.PrefetchScalarGridSpec(
        num_scalar_prefetch=1,
        grid=(n_tiles,),
        in_specs=[
            pl.BlockSpec((None, 1, MOE_TM), lambda t, nu: (t, 0, 0),
                         memory_space=pltpu.SMEM),
            pl.BlockSpec(memory_space=pl.ANY),
        ],
        out_specs=pl.BlockSpec((MOE_TM, D_MODEL), lambda t, nu: (t, 0)),
        scratch_shapes=[
            pltpu.VMEM((MOE_TM * ROW_CHUNKS, LANES), F32),
            pltpu.SemaphoreType.DMA(()),
        ],
    )
    return pl.pallas_call(
        _gather_body,
        grid_spec=grid_spec,
        out_shape=jax.ShapeDtypeStruct((n_tiles * MOE_TM, D_MODEL), BF16),
        compiler_params=_params(("arbitrary",)),
        name="moe_gather",
    )(n_used, row_tok.reshape(n_tiles, 1, MOE_TM), hn_chunks)


MOE_ITEM_TILES = 8
MOE_ITEM_ROWS = MOE_ITEM_TILES * MOE_TM
MOE_TF = 256
MOE_NF = D_FF // MOE_TF


def _moe_body(e_ref, row0_ref, nt_ref, nused_ref, xs_hbm, wg_ref, wu_ref, bg_ref, bu_ref, wd_ref, bd_ref,
              y_hbm, xmat, yacc, wg_s, wu_s, wd_s, ystage, sem_in, sem_out):
    w = pl.program_id(0)
    f = pl.program_id(1)
    nt = nt_ref[w]
    row0 = row0_ref[w]

    def x_copy(r):
        return pltpu.make_async_copy(
            xs_hbm.at[pl.ds(pl.multiple_of(row0 + r * MOE_TM, MOE_TM), MOE_TM), :],
            xmat.at[pl.ds(r * MOE_TM, MOE_TM), :], sem_in)

    @pl.when(nt > 0)
    def _():
        @pl.when(f == 0)
        def _():
            for r in range(MOE_ITEM_TILES):
                @pl.when(r < nt)
                def _():
                    x_copy(r).start()
            for r in range(MOE_ITEM_TILES):
                @pl.when(r < nt)
                def _():
                    x_copy(r).wait()

        wg_s[...] = wg_ref[...].astype(BF16)
        wu_s[...] = wu_ref[...].astype(BF16)
        wd_s[...] = wd_ref[...].astype(BF16)

        def tile(r, carry):
            rows = pl.ds(pl.multiple_of(r * MOE_TM, MOE_TM), MOE_TM)
            x = xmat[rows, :]
            gate = jnp.dot(x, wg_s[...], preferred_element_type=F32) + bg_ref[...]
            up = jnp.dot(x, wu_s[...], preferred_element_type=F32) + bu_ref[...]
            gate = jnp.minimum(gate, SWIGLU_LIMIT)
            up = jnp.clip(up, -SWIGLU_LIMIT, SWIGLU_LIMIT)
            act = gate * jax.nn.sigmoid(SWIGLU_ALPHA * gate) * (up + 1.0)
            part = jnp.dot(act.astype(BF16), wd_s[...], preferred_element_type=F32)

            @pl.when(f == 0)
            def _():
                yacc[rows, :] = part + bd_ref[...]

            @pl.when(f > 0)
            def _():
                yacc[rows, :] += part

            return carry

        lax.fori_loop(0, nt, tile, 0)

        @pl.when(f == MOE_NF - 1)
        def _():
            def flush(r, carry):
                rows = pl.ds(pl.multiple_of(r * MOE_TM, MOE_TM), MOE_TM)
                for s in range(ROW_CHUNKS):
                    ystage[pl.ds(s, MOE_TM, stride=ROW_CHUNKS), :] = \
                        yacc[rows, s * LANES:(s + 1) * LANES]
                dst0 = pl.multiple_of((row0 + r * MOE_TM) * ROW_CHUNKS, MOE_TM * ROW_CHUNKS)
                cp = pltpu.make_async_copy(
                    ystage, y_hbm.at[pl.ds(dst0, MOE_TM * ROW_CHUNKS), :], sem_out)
                cp.start()
                cp.wait()
                return carry

            lax.fori_loop(0, nt, flush, 0)

    @pl.when((w == pl.num_programs(0) - 1) & (f == MOE_NF - 1))
    def _():
        ystage[...] = jnp.zeros(ystage.shape, ystage.dtype)

        def zero_tile(tix, carry):
            dst0 = pl.multiple_of(tix * (MOE_TM * ROW_CHUNKS), MOE_TM * ROW_CHUNKS)
            cp = pltpu.make_async_copy(
                ystage, y_hbm.at[pl.ds(dst0, MOE_TM * ROW_CHUNKS), :], sem_out)
            cp.start()
            cp.wait()
            return carry

        lax.fori_loop(nused_ref[0], y_hbm.shape[0] // (MOE_TM * ROW_CHUNKS), zero_tile, 0)


def _moe_ffn(item_e, item_row0, item_nt, n_used, xs, w_gate_up, b_gate_up, w_down, b_down, n_items):
    n_rows = xs.shape[0]

    def fidx(w, f, nt):
        return jnp.where(nt[w] > 0, f, MOE_NF - 1)

    grid_spec = pltpu.PrefetchScalarGridSpec(
        num_scalar_prefetch=4,
        grid=(n_items, MOE_NF),
        in_specs=[
            pl.BlockSpec(memory_space=pl.ANY),
            pl.BlockSpec((None, D_MODEL, MOE_TF), lambda w, f, e, r0, nt, nu: (e[w], 0, fidx(w, f, nt))),
            pl.BlockSpec((None, D_MODEL, MOE_TF),
                         lambda w, f, e, r0, nt, nu: (e[w], 0, MOE_NF + fidx(w, f, nt))),
            pl.BlockSpec((None, 1, MOE_TF), lambda w, f, e, r0, nt, nu: (e[w], 0, fidx(w, f, nt))),
            pl.BlockSpec((None, 1, MOE_TF),
                         lambda w, f, e, r0, nt, nu: (e[w], 0, MOE_NF + fidx(w, f, nt))),
            pl.BlockSpec((None, MOE_TF, D_MODEL), lambda w, f, e, r0, nt, nu: (e[w], fidx(w, f, nt), 0)),
            pl.BlockSpec((None, 1, D_MODEL), lambda w, f, e, r0, nt, nu: (e[w], 0, 0)),
        ],
        out_specs=pl.BlockSpec(memory_space=pl.ANY),
        scratch_shapes=[
            pltpu.VMEM((MOE_ITEM_ROWS, D_MODEL), BF16),
            pltpu.VMEM((MOE_ITEM_ROWS, D_MODEL), F32),
            pltpu.VMEM((D_MODEL, MOE_TF), BF16),
            pltpu.VMEM((D_MODEL, MOE_TF), BF16),
            pltpu.VMEM((MOE_TF, D_MODEL), BF16),
            pltpu.VMEM((MOE_TM * ROW_CHUNKS, LANES), F32),
            pltpu.SemaphoreType.DMA(()),
            pltpu.SemaphoreType.DMA(()),
        ],
    )
    return pl.pallas_call(
        _moe_body,
        grid_spec=grid_spec,
        out_shape=jax.ShapeDtypeStruct((n_rows * ROW_CHUNKS, LANES), F32),
        compiler_params=_params(("arbitrary", "arbitrary")),
        name="moe_ffn",
    )(item_e, item_row0, item_nt, n_used, xs, w_gate_up, w_gate_up,
      b_gate_up.reshape(N_EXPERTS, 1, 2 * D_FF), b_gate_up.reshape(N_EXPERTS, 1, 2 * D_FF),
      w_down, b_down.reshape(N_EXPERTS, 1, D_MODEL))


COMB_TM = 128


def _combine_body(dest_ref, y_hbm, gate_ref, h_ref, g_ref, o_ref, stage, ff, sem):
    def issue(r, carry):
        for k in range(TOP_K):
            _row_copy(y_hbm, stage.at[k], dest_ref[0, r * TOP_K + k], r, sem).start()
        return carry

    lax.fori_loop(0, COMB_TM, issue, 0)

    def drain(r, carry):
        for k in range(TOP_K):
            _row_copy(y_hbm, stage.at[k], dest_ref[0, r * TOP_K + k], r, sem).wait()
        return carry

    lax.fori_loop(0, COMB_TM, drain, 0)

    gates = gate_ref[...]
    for s in range(ROW_CHUNKS):
        acc = gates[:, 0:1] * stage[0, pl.ds(s, COMB_TM, stride=ROW_CHUNKS), :]
        for k in range(1, TOP_K):
            acc = acc + gates[:, k:k + 1] * stage[k, pl.ds(s, COMB_TM, stride=ROW_CHUNKS), :]
        ff[:, s * LANES:(s + 1) * LANES] = acc
    h = h_ref[...] + ff[...]
    inv = lax.rsqrt(jnp.mean(h * h, axis=-1, keepdims=True) + NORM_EPS)
    o_ref[...] = h * inv * g_ref[...]


def _combine(dest, y_chunks, gates, h1, g):
    t_real, d = h1.shape
    n_tiles = t_real // COMB_TM
    return pl.pallas_call(
        _combine_body,
        grid=(n_tiles,),
        in_specs=[
            pl.BlockSpec((None, 1, COMB_TM * TOP_K), lambda i: (i, 0, 0),
                         memory_space=pltpu.SMEM),
            pl.BlockSpec(memory_space=pl.ANY),
            pl.BlockSpec((COMB_TM, LANES), lambda i: (i, 0)),
            pl.BlockSpec((COMB_TM, d), lambda i: (i, 0)),
            pl.BlockSpec((1, d), lambda i: (0, 0)),
        ],
        out_specs=pl.BlockSpec((COMB_TM, d), lambda i: (i, 0)),
        out_shape=jax.ShapeDtypeStruct((t_real, d), F32),
        scratch_shapes=[
            pltpu.VMEM((TOP_K, COMB_TM * ROW_CHUNKS, LANES), F32),
            pltpu.VMEM((COMB_TM, d), F32),
            pltpu.SemaphoreType.DMA(()),
        ],
        compiler_params=_params(("arbitrary",)),
        name="moe_combine",
    )(dest.reshape(n_tiles, 1, COMB_TM * TOP_K), y_chunks, gates, h1, g)


def _routing_plan(top_i, n_tiles, n_items):
    t_real = top_i.shape[0]
    flat_e = top_i.reshape(-1)
    onehot = (flat_e[:, None] == jnp.arange(N_EXPERTS, dtype=jnp.int32)[None, :]).astype(jnp.int32)
    incl = jnp.cumsum(onehot, axis=0)
    rank = jnp.sum((incl - onehot) * onehot, axis=1)
    counts = incl[-1]
    tiles_e = (counts + MOE_TM - 1) // MOE_TM
    pad_end = jnp.cumsum(tiles_e) * MOE_TM
    pad_start = pad_end - tiles_e * MOE_TM
    dest = pad_start[flat_e] + rank
    flat_tok = jnp.arange(t_real * TOP_K, dtype=jnp.int32) // TOP_K
    row_tok = jnp.zeros((n_tiles * MOE_TM,), jnp.int32).at[dest].set(flat_tok)
    n_used = (pad_end[-1] // MOE_TM).astype(jnp.int32).reshape(1)

    items_e = (tiles_e + MOE_ITEM_TILES - 1) // MOE_ITEM_TILES
    item_end = jnp.cumsum(items_e)
    item_start = item_end - items_e
    wid = jnp.arange(n_items, dtype=jnp.int32)
    e_of = jnp.minimum(jnp.searchsorted(item_end, wid, side="right"), N_EXPERTS - 1).astype(jnp.int32)
    k_of = wid - item_start[e_of]
    live = wid < item_end[-1]
    last_e = jnp.max(jnp.where(counts > 0, jnp.arange(N_EXPERTS, dtype=jnp.int32), 0))
    item_e = jnp.where(live, e_of, last_e).astype(jnp.int32)
    item_nt = jnp.where(live, jnp.minimum(MOE_ITEM_TILES, tiles_e[e_of] - k_of * MOE_ITEM_TILES), 0)
    item_row0 = jnp.where(live, pad_start[e_of] + k_of * MOE_ITEM_ROWS, 0)
    return (row_tok, n_used, dest.astype(jnp.int32), item_e,
            item_row0.astype(jnp.int32), item_nt.astype(jnp.int32))


def kernel(x, meta_tokens, norm_mix_g, w_in, b_fgate, b_glu, conv_w, conv_b, gn_g, gn_b, w_out,
           norm_ffn_g, router_w, router_b, w_gate_up, b_gate_up, w_down, b_down, norm_final_g):
    assert x.shape[0] == 1 and norm_mix_g.shape[0] == 1
    xt = x[0]
    t_real = xt.shape[0]
    fcol = 3 * ATT_WIDTH
    w_in0 = w_in[0]
    w_main = jnp.concatenate([w_in0[:, :fcol], w_in0[:, fcol + N_HEADS:]], axis=1).astype(BF16)
    w_f = jnp.pad(w_in0[:, fcol:fcol + N_HEADS], ((0, 0), (0, LANES - N_HEADS))).astype(BF16)
    g_mix = norm_mix_g[0].reshape(1, D_MODEL)

    proj, f_real = _inproj(xt, g_mix, w_main, w_f, tm=1024, tn=512)
    proj_meta, f_meta = _inproj(meta_tokens, g_mix, w_main, w_f, tm=N_META, tn=512)

    f_all = jnp.concatenate([f_meta[:, :N_HEADS], f_real[:, :N_HEADS]], axis=0)
    c_all = jnp.cumsum(jax.nn.log_sigmoid(f_all + b_fgate[0][None, :]), axis=0)
    c_t = c_all.T
    cq = c_t[:, N_META:, None]
    ck = c_t[:, None, N_META:]
    ckm = c_t[:, None, :N_META]

    att = _attention(proj, proj_meta, cq, ck, ckm, tq=512)
    conv = _conv_mixer(proj, proj_meta, b_glu[0].reshape(1, -1), conv_w[0],
                       conv_b[0].reshape(1, -1), gn_g[0].reshape(1, -1), gn_b[0].reshape(1, -1),
                       tm=512)

    rw = jnp.pad(router_w[0], ((0, 0), (0, LANES - N_EXPERTS)))
    rb = jnp.pad(router_b[0], (0, LANES - N_EXPERTS), constant_values=NEG_BIG).reshape(1, LANES)
    h1, hn_chunks, top_i, gates = _outproj_router(
        att, conv, xt, w_out[0].astype(BF16), norm_ffn_g[0].reshape(1, D_MODEL), rw, rb, tm=256)

    n_assign = t_real * TOP_K
    n_tiles = (n_assign + N_EXPERTS * (MOE_TM - 1)) // MOE_TM + 1
    n_items = N_EXPERTS + (n_tiles * MOE_TM) // MOE_ITEM_ROWS
    row_tok, n_used, dest, item_e, item_row0, item_nt = _routing_plan(
        top_i[:, :TOP_K], n_tiles, n_items)

    xs = _gather_rows(n_used, row_tok, hn_chunks, n_tiles)
    y_chunks = _moe_ffn(item_e, item_row0, item_nt, n_used, xs, w_gate_up[0], b_gate_up[0],
                        w_down[0], b_down[0], n_items)
    out = _combine(dest, y_chunks, gates, h1, norm_final_g.reshape(1, D_MODEL))
    return out[None]
```

```python
import functools

import jax
import jax.numpy as jnp
import numpy as np
from jax import lax
from jax.experimental import pallas as pl
from jax.experimental.pallas import tpu as pltpu

D_MODEL = 2048
N_META = 16
HEAD_DIM = 128
ATT_WIDTH = 1024
N_HEADS = ATT_WIDTH // HEAD_DIM
CONV_CH = 1024
N_GROUPS = 8
GROUP_CH = CONV_CH // N_GROUPS
CONV_K = 31
N_EXPERTS = 32
TOP_K = 4
D_FF = 2048
SWIGLU_LIMIT = 7.0
SWIGLU_ALPHA = 1.702
NORM_EPS = 1e-5
NEG_BIG = -1e30

LANES = 128
ROW_CHUNKS = D_MODEL // LANES
VMEM_LIMIT = 56 * 1024 * 1024

F32 = jnp.float32
BF16 = jnp.bfloat16


def _params(sem, vmem=VMEM_LIMIT):
    return pltpu.CompilerParams(dimension_semantics=sem, vmem_limit_bytes=vmem)


def _inproj_body(x_ref, g_ref, w_ref, wf_ref, o_ref, f_ref, hn_ref):
    @pl.when(pl.program_id(1) == 0)
    def _():
        x = x_ref[...]
        inv = lax.rsqrt(jnp.mean(x * x, axis=-1, keepdims=True) + NORM_EPS)
        hn = (x * inv * g_ref[...]).astype(BF16)
        hn_ref[...] = hn
        f_ref[...] = jnp.dot(hn, wf_ref[...], preferred_element_type=F32)

    o_ref[...] = jnp.dot(hn_ref[...], w_ref[...], preferred_element_type=F32).astype(o_ref.dtype)


def _inproj(x, g, w_main, w_f, tm, tn):
    m, d = x.shape
    n = w_main.shape[1]
    return pl.pallas_call(
        _inproj_body,
        grid=(m // tm, n // tn),
        in_specs=[
            pl.BlockSpec((tm, d), lambda i, j: (i, 0)),
            pl.BlockSpec((1, d), lambda i, j: (0, 0)),
            pl.BlockSpec((d, tn), lambda i, j: (0, j)),
            pl.BlockSpec((d, LANES), lambda i, j: (0, 0)),
        ],
        out_specs=[
            pl.BlockSpec((tm, tn), lambda i, j: (i, j)),
            pl.BlockSpec((tm, LANES), lambda i, j: (i, 0)),
        ],
        out_shape=[
            jax.ShapeDtypeStruct((m, n), BF16),
            jax.ShapeDtypeStruct((m, LANES), F32),
        ],
        scratch_shapes=[pltpu.VMEM((tm, d), BF16)],
        compiler_params=_params(("arbitrary", "arbitrary")),
        name="inproj",
    )(x, g, w_main, w_f)


ATT_SCALE = 1.0 / float(np.sqrt(HEAD_DIM))
EXP2_MULT = ATT_SCALE * float(np.log2(np.e))
N_SPLIT = 3
FGATE_TB = 512


def _log_sigmoid(x):
    return jnp.minimum(x, 0.0) - jnp.log1p(jnp.exp(-jnp.abs(x)))


def _prefix_rows(lf):
    n = lf.shape[0]
    row = lax.broadcasted_iota(jnp.int32, (n, n), 0)
    col = lax.broadcasted_iota(jnp.int32, (n, n), 1)
    tri = (col <= row).astype(F32)
    return jnp.dot(tri, lf, preferred_element_type=F32, precision=lax.Precision.HIGHEST)


def _bias_columns(c, head):
    lane = lax.broadcasted_iota(jnp.int32, c.shape, 1)
    x = jnp.sum(jnp.where(lane == head, c, 0.0), axis=-1, keepdims=True) * float(np.sqrt(HEAD_DIM))
    hi = x.astype(BF16).astype(F32)
    r1 = x - hi
    mid = r1.astype(BF16).astype(F32)
    lo = r1 - mid
    part = lane % N_SPLIT
    parts = jnp.where(part == 0, hi, jnp.where(part == 1, mid, lo))
    qa = jnp.where(lane < N_SPLIT, parts, jnp.where(lane < 2 * N_SPLIT, 1.0, 0.0))
    ka = jnp.where(lane < N_SPLIT, 1.0, jnp.where(lane < 2 * N_SPLIT, -parts, 0.0))
    return qa.astype(BF16), ka.astype(BF16)


def _fgate_body(fm_ref, f_ref, b_ref, qa_ref, ka_ref, kam_ref, carry_ref):
    @pl.when(pl.program_id(0) == 0)
    def _():
        cm = _prefix_rows(_log_sigmoid(fm_ref[...] + b_ref[...]))
        carry_ref[...] = cm[N_META - 1:N_META, :]
        for head in range(N_HEADS):
            kam_ref[head] = _bias_columns(cm, head)[1]

    c = _prefix_rows(_log_sigmoid(f_ref[...] + b_ref[...])) + carry_ref[...]
    carry_ref[...] = c[FGATE_TB - 1:FGATE_TB, :]
    for head in range(N_HEADS):
        qa, ka = _bias_columns(c, head)
        qa_ref[head] = qa
        ka_ref[head] = ka


def _fgate_bias(f_meta, f_real, b_f):
    t_real = f_real.shape[0]
    return pl.pallas_call(
        _fgate_body,
        grid=(t_real // FGATE_TB,),
        in_specs=[
            pl.BlockSpec((N_META, LANES), lambda i: (0, 0)),
            pl.BlockSpec((FGATE_TB, LANES), lambda i: (i, 0)),
            pl.BlockSpec((1, LANES), lambda i: (0, 0)),
        ],
        out_specs=[
            pl.BlockSpec((N_HEADS, FGATE_TB, LANES), lambda i: (0, i, 0)),
            pl.BlockSpec((N_HEADS, FGATE_TB, LANES), lambda i: (0, i, 0)),
            pl.BlockSpec((N_HEADS, N_META, LANES), lambda i: (0, 0, 0)),
        ],
        out_shape=[
            jax.ShapeDtypeStruct((N_HEADS, t_real, LANES), BF16),
            jax.ShapeDtypeStruct((N_HEADS, t_real, LANES), BF16),
            jax.ShapeDtypeStruct((N_HEADS, N_META, LANES), BF16),
        ],
        scratch_shapes=[pltpu.VMEM((1, LANES), F32)],
        compiler_params=_params(("arbitrary",)),
        name="fgate_bias",
    )(f_meta, f_real, b_f)


def _attn_body(q_ref, qa_ref, k_ref, ka_ref, v_ref, km_ref, kam_ref, vmt_ref, o_ref, vt_ref, *, tq):
    i = pl.program_id(1)
    n_blocks = vt_ref.shape[0]
    contract_last = (((1,), (1,)), ((), ()))

    @pl.when(i == 0)
    def _():
        for blk in range(n_blocks):
            vt_ref[blk] = v_ref[blk * tq:(blk + 1) * tq, :].T

    q = jnp.concatenate([q_ref[...], qa_ref[...]], axis=1)

    def scores_t(k, ka):
        return lax.dot_general(jnp.concatenate([k, ka], axis=1), q, contract_last,
                               preferred_element_type=F32)

    s = scores_t(km_ref[...], kam_ref[...])
    m = jnp.max(s, axis=0, keepdims=True)
    p = jnp.exp2((s - m) * EXP2_MULT)
    l = jnp.sum(p, axis=0, keepdims=True)
    acc = jnp.dot(vmt_ref[...], p.astype(BF16), preferred_element_type=F32)

    def block_scores(j):
        rows = pl.ds(pl.multiple_of(j * tq, tq), tq)
        return scores_t(k_ref[rows, :], ka_ref[rows, :])

    def absorb(j, s, m_prev, l_prev, acc_prev):
        m_new = jnp.maximum(m_prev, jnp.max(s, axis=0, keepdims=True))
        alpha = jnp.exp2((m_prev - m_new) * EXP2_MULT)
        p = jnp.exp2((s - m_new) * EXP2_MULT)
        l_new = alpha * l_prev + jnp.sum(p, axis=0, keepdims=True)
        acc_new = alpha * acc_prev + jnp.dot(vt_ref[j], p.astype(BF16),
                                             preferred_element_type=F32)
        return m_new, l_new, acc_new

    def step(j, carry):
        s_cur, m_prev, l_prev, acc_prev = carry
        s_next = block_scores(j + 1)
        return (s_next,) + absorb(j, s_cur, m_prev, l_prev, acc_prev)

    s_diag, m, l, acc = lax.fori_loop(0, i, step, (block_scores(0), m, l, acc))
    key = lax.broadcasted_iota(jnp.int32, s_diag.shape, 0)
    qry = lax.broadcasted_iota(jnp.int32, s_diag.shape, 1)
    m, l, acc = absorb(i, jnp.where(key <= qry, s_diag, NEG_BIG), m, l, acc)
    o_ref[...] = (acc / l).T.astype(o_ref.dtype)


def _attention(proj, proj_meta, vm_t, qa, ka, kam, tq):
    t_real = proj.shape[0]
    kcol = ATT_WIDTH // HEAD_DIM
    vcol = 2 * ATT_WIDTH // HEAD_DIM
    return pl.pallas_call(
        functools.partial(_attn_body, tq=tq),
        grid=(N_HEADS, t_real // tq),
        in_specs=[
            pl.BlockSpec((tq, HEAD_DIM), lambda h, i: (i, h)),
            pl.BlockSpec((None, tq, LANES), lambda h, i: (h, i, 0)),
            pl.BlockSpec((t_real, HEAD_DIM), lambda h, i: (0, kcol + h)),
            pl.BlockSpec((None, t_real, LANES), lambda h, i: (h, 0, 0)),
            pl.BlockSpec((t_real, HEAD_DIM), lambda h, i: (0, vcol + h)),
            pl.BlockSpec((N_META, HEAD_DIM), lambda h, i: (0, kcol + h)),
            pl.BlockSpec((None, N_META, LANES), lambda h, i: (h, 0, 0)),
            pl.BlockSpec((None, HEAD_DIM, N_META), lambda h, i: (h, 0, 0)),
        ],
        out_specs=pl.BlockSpec((tq, HEAD_DIM), lambda h, i: (i, h)),
        out_shape=jax.ShapeDtypeStruct((t_real, ATT_WIDTH), BF16),
        scratch_shapes=[pltpu.VMEM((t_real // tq, HEAD_DIM, tq), BF16)],
        compiler_params=_params(("arbitrary", "arbitrary")),
        name="fox_attention",
    )(proj, qa, proj, ka, proj, proj_meta, kam, vm_t)


CONV_HIST = 32
CONV_ROWS = 128


def _glu(a_ref, g_ref, bglu_ref):
    a = a_ref[...].astype(F32) + bglu_ref[:, :CONV_CH]
    g = g_ref[...].astype(F32) + bglu_ref[:, CONV_CH:]
    return a * jax.nn.sigmoid(g)


def _conv_body(a_ref, g_ref, am_ref, gm_ref, bglu_ref, w_ref, cb_ref, gng_ref, gnb_ref,
               o_ref, ubuf, *, tm):
    i = pl.program_id(0)

    @pl.when(i == 0)
    def _():
        ubuf[0:CONV_HIST - N_META, :] = jnp.zeros((CONV_HIST - N_META, CONV_CH), F32)
        ubuf[CONV_HIST - N_META:CONV_HIST, :] = _glu(am_ref, gm_ref, bglu_ref)

    @pl.when(i > 0)
    def _():
        ubuf[0:CONV_HIST, :] = ubuf[tm:tm + CONV_HIST, :]

    ubuf[CONV_HIST:CONV_HIST + tm, :] = _glu(a_ref, g_ref, bglu_ref)

    first = CONV_HIST - (CONV_K - 1)
    for grp in range(N_GROUPS):
        lanes = slice(grp * GROUP_CH, (grp + 1) * GROUP_CH)
        for rc in range(tm // CONV_ROWS):
            base = rc * CONV_ROWS
            acc = jnp.zeros((CONV_ROWS, GROUP_CH), F32)
            for tap in range(CONV_K):
                acc = acc + ubuf[base + first + tap:base + first + tap + CONV_ROWS, lanes] * \
                    w_ref[tap:tap + 1, lanes]
            acc = acc + cb_ref[:, lanes]
            mu = jnp.mean(acc, axis=-1, keepdims=True)
            dlt = acc - mu
            var = jnp.mean(dlt * dlt, axis=-1, keepdims=True)
            y = dlt * lax.rsqrt(var + NORM_EPS) * gng_ref[:, lanes] + gnb_ref[:, lanes]
            o_ref[base:base + CONV_ROWS, lanes] = (y * jax.nn.sigmoid(y)).astype(o_ref.dtype)


def _conv_mixer(proj, proj_meta, b_glu, conv_w, conv_b, gn_g, gn_b, tm):
    t_real = proj.shape[0]
    acol = 3 * ATT_WIDTH // CONV_CH
    gcol = acol + 1
    const = lambda i: (0, 0)
    return pl.pallas_call(
        functools.partial(_conv_body, tm=tm),
        grid=(t_real // tm,),
        in_specs=[
            pl.BlockSpec((tm, CONV_CH), lambda i: (i, acol)),
            pl.BlockSpec((tm, CONV_CH), lambda i: (i, gcol)),
            pl.BlockSpec((N_META, CONV_CH), lambda i: (0, acol)),
            pl.BlockSpec((N_META, CONV_CH), lambda i: (0, gcol)),
            pl.BlockSpec((1, 2 * CONV_CH), const),
            pl.BlockSpec((CONV_K, CONV_CH), const),
            pl.BlockSpec((1, CONV_CH), const),
            pl.BlockSpec((1, CONV_CH), const),
            pl.BlockSpec((1, CONV_CH), const),
        ],
        out_specs=pl.BlockSpec((tm, CONV_CH), lambda i: (i, 0)),
        out_shape=jax.ShapeDtypeStruct((t_real, CONV_CH), BF16),
        scratch_shapes=[pltpu.VMEM((tm + CONV_HIST, CONV_CH), F32)],
        compiler_params=_params(("arbitrary",)),
        name="conv_mixer",
    )(proj, proj, proj_meta, proj_meta, b_glu, conv_w, conv_b, gn_g, gn_b)


def _outproj_body(att_ref, conv_ref, x_ref, w_ref, g_ref, rw_ref, rb_ref,
                  h_ref, hnc_ref, topi_ref, gate_ref, *, tm):
    mix = jnp.dot(att_ref[...], w_ref[0:ATT_WIDTH, :], preferred_element_type=F32)
    mix = mix + jnp.dot(conv_ref[...], w_ref[ATT_WIDTH:, :], preferred_element_type=F32)
    h = x_ref[...] + mix
    h_ref[...] = h
    inv = lax.rsqrt(jnp.mean(h * h, axis=-1, keepdims=True) + NORM_EPS)
    hn = h * inv * g_ref[...]
    for s in range(ROW_CHUNKS):
        hnc_ref[pl.ds(s, tm, stride=ROW_CHUNKS), :] = hn[:, s * LANES:(s + 1) * LANES]

    logits = jnp.dot(hn, rw_ref[...], preferred_element_type=F32,
                     precision=lax.Precision.HIGHEST) + rb_ref[...]
    lane = lax.broadcasted_iota(jnp.int32, logits.shape, 1)
    vals = logits
    tops = []
    idxs = []
    for _ in range(TOP_K):
        mx = jnp.max(vals, axis=-1, keepdims=True)
        ix = jnp.min(jnp.where(vals == mx, lane, LANES), axis=-1, keepdims=True)
        tops.append(mx)
        idxs.append(ix)
        vals = jnp.where(lane == ix, -jnp.inf, vals)
    exps = [jnp.exp(v - tops[0]) for v in tops]
    denom = exps[0] + exps[1] + exps[2] + exps[3]
    topi = jnp.zeros(logits.shape, jnp.int32)
    gate = jnp.zeros(logits.shape, F32)
    for k in range(TOP_K):
        topi = jnp.where(lane == k, idxs[k], topi)
        gate = jnp.where(lane == k, exps[k] / denom, gate)
    topi_ref[...] = topi
    gate_ref[...] = gate


def _outproj_router(att, conv, x, w_out, g, router_w, router_b, tm):
    t_real, d = x.shape
    const = lambda i: (0, 0)
    return pl.pallas_call(
        functools.partial(_outproj_body, tm=tm),
        grid=(t_real // tm,),
        in_specs=[
            pl.BlockSpec((tm, ATT_WIDTH), lambda i: (i, 0)),
            pl.BlockSpec((tm, CONV_CH), lambda i: (i, 0)),
            pl.BlockSpec((tm, d), lambda i: (i, 0)),
            pl.BlockSpec((d, d), const),
            pl.BlockSpec((1, d), const),
            pl.BlockSpec((d, LANES), const),
            pl.BlockSpec((1, LANES), const),
        ],
        out_specs=[
            pl.BlockSpec((tm, d), lambda i: (i, 0)),
            pl.BlockSpec((tm * ROW_CHUNKS, LANES), lambda i: (i, 0)),
            pl.BlockSpec((tm, LANES), lambda i: (i, 0)),
            pl.BlockSpec((tm, LANES), lambda i: (i, 0)),
        ],
        out_shape=[
            jax.ShapeDtypeStruct((t_real, d), F32),
            jax.ShapeDtypeStruct((t_real * ROW_CHUNKS, LANES), F32),
            jax.ShapeDtypeStruct((t_real, LANES), jnp.int32),
            jax.ShapeDtypeStruct((t_real, LANES), F32),
        ],
        compiler_params=_params(("arbitrary",)),
        name="outproj_router",
    )(att, conv, x, w_out, g, router_w, router_b)


MOE_TM = 256


def _row_copy(src_hbm, dst, tok, slot, sem):
    return pltpu.make_async_copy(
        src_hbm.at[pl.ds(pl.multiple_of(tok * ROW_CHUNKS, ROW_CHUNKS), ROW_CHUNKS), :],
        dst.at[pl.ds(pl.multiple_of(slot * ROW_CHUNKS, ROW_CHUNKS), ROW_CHUNKS), :],
        sem)


def _gather_body(nused_ref, tok_ref, src_hbm, o_ref, stage, sem):
    t = pl.program_id(0)

    @pl.when(t < nused_ref[0])
    def _():
        def issue(r, carry):
            _row_copy(src_hbm, stage, tok_ref[0, r], r, sem).start()
            return carry

        lax.fori_loop(0, MOE_TM, issue, 0)

        def drain(r, carry):
            _row_copy(src_hbm, stage, tok_ref[0, r], r, sem).wait()
            return carry

        lax.fori_loop(0, MOE_TM, drain, 0)
        for s in range(ROW_CHUNKS):
            o_ref[:, s * LANES:(s + 1) * LANES] = \
                stage[pl.ds(s, MOE_TM, stride=ROW_CHUNKS), :].astype(o_ref.dtype)

    @pl.when(t >= nused_ref[0])
    def _():
        o_ref[...] = jnp.zeros(o_ref.shape, o_ref.dtype)


def _gather_rows(n_used, row_tok, hn_chunks, n_tiles):
    grid_spec = pltpu.PrefetchScalarGridSpec(
        num_scalar_prefetch=1,
        grid=(n_tiles,),
        in_specs=[
            pl.BlockSpec((None, 1, MOE_TM), lambda t, nu: (t, 0, 0),
                         memory_space=pltpu.SMEM),
            pl.BlockSpec(memory_space=pl.ANY),
        ],
        out_specs=pl.BlockSpec((MOE_TM, D_MODEL), lambda t, nu: (t, 0)),
        scratch_shapes=[
            pltpu.VMEM((MOE_TM * ROW_CHUNKS, LANES), F32),
            pltpu.SemaphoreType.DMA(()),
        ],
    )
    return pl.pallas_call(
        _gather_body,
        grid_spec=grid_spec,
        out_shape=jax.ShapeDtypeStruct((n_tiles * MOE_TM, D_MODEL), BF16),
        compiler_params=_params(("arbitrary",)),
        name="moe_gather",
    )(n_used, row_tok.reshape(n_tiles, 1, MOE_TM), hn_chunks)


MOE_ITEM_TILES = 8
MOE_ITEM_ROWS = MOE_ITEM_TILES * MOE_TM
MOE_TF = 256
MOE_NF = D_FF // MOE_TF


def _moe_body(e_ref, row0_ref, nt_ref, nused_ref, xs_hbm, wg_ref, wu_ref, bg_ref, bu_ref, wd_ref, bd_ref,
              y_hbm, xmat, yacc, ystage, sem_in, sem_out):
    w = pl.program_id(0)
    f = pl.program_id(1)
    nt = nt_ref[w]
    row0 = row0_ref[w]

    def x_copy(r):
        return pltpu.make_async_copy(
            xs_hbm.at[pl.ds(pl.multiple_of(row0 + r * MOE_TM, MOE_TM), MOE_TM), :],
            xmat.at[pl.ds(r * MOE_TM, MOE_TM), :], sem_in)

    @pl.when(nt > 0)
    def _():
        @pl.when(f == 0)
        def _():
            for r in range(MOE_ITEM_TILES):
                @pl.when(r < nt)
                def _():
                    x_copy(r).start()
            for r in range(MOE_ITEM_TILES):
                @pl.when(r < nt)
                def _():
                    x_copy(r).wait()

        def tile_rows(r):
            return pl.ds(pl.multiple_of(r * MOE_TM, MOE_TM), MOE_TM)

        def hidden(r):
            x = xmat[tile_rows(r), :]
            gate = jnp.dot(x, wg_ref[...].astype(BF16), preferred_element_type=F32) + bg_ref[...]
            up = jnp.dot(x, wu_ref[...].astype(BF16), preferred_element_type=F32) + bu_ref[...]
            gate = jnp.minimum(gate, SWIGLU_LIMIT)
            up = jnp.clip(up, -SWIGLU_LIMIT, SWIGLU_LIMIT)
            return (gate * jax.nn.sigmoid(SWIGLU_ALPHA * gate) * (up + 1.0)).astype(BF16)

        def project(r, act):
            yacc[tile_rows(r), :] += jnp.dot(act, wd_ref[...].astype(BF16),
                                             preferred_element_type=F32)

        def tile(r, act_prev):
            act = hidden(r)
            project(r - 1, act_prev)
            return act

        @pl.when(f == 0)
        def _():
            def seed(r, carry):
                rows = pl.ds(pl.multiple_of(r * MOE_TM, MOE_TM), MOE_TM)
                yacc[rows, :] = jnp.broadcast_to(bd_ref[...], (MOE_TM, D_MODEL))
                return carry

            lax.fori_loop(0, nt, seed, 0)

        project(nt - 1, lax.fori_loop(1, nt, tile, hidden(0)))

        @pl.when(f == MOE_NF - 1)
        def _():
            def flush(r, carry):
                rows = pl.ds(pl.multiple_of(r * MOE_TM, MOE_TM), MOE_TM)
                for s in range(ROW_CHUNKS):
                    ystage[pl.ds(s, MOE_TM, stride=ROW_CHUNKS), :] = \
                        yacc[rows, s * LANES:(s + 1) * LANES]
                dst0 = pl.multiple_of((row0 + r * MOE_TM) * ROW_CHUNKS, MOE_TM * ROW_CHUNKS)
                cp = pltpu.make_async_copy(
                    ystage, y_hbm.at[pl.ds(dst0, MOE_TM * ROW_CHUNKS), :], sem_out)
                cp.start()
                cp.wait()
                return carry

            lax.fori_loop(0, nt, flush, 0)

    @pl.when((w == pl.num_programs(0) - 1) & (f == MOE_NF - 1))
    def _():
        ystage[...] = jnp.zeros(ystage.shape, ystage.dtype)

        def zero_tile(tix, carry):
            dst0 = pl.multiple_of(tix * (MOE_TM * ROW_CHUNKS), MOE_TM * ROW_CHUNKS)
            cp = pltpu.make_async_copy(
                ystage, y_hbm.at[pl.ds(dst0, MOE_TM * ROW_CHUNKS), :], sem_out)
            cp.start()
            cp.wait()
            return carry

        lax.fori_loop(nused_ref[0], y_hbm.shape[0] // (MOE_TM * ROW_CHUNKS), zero_tile, 0)


def _moe_ffn(item_e, item_row0, item_nt, n_used, xs, w_gate_up, b_gate_up, w_down, b_down, n_items):
    n_rows = xs.shape[0]

    def fidx(w, f, nt):
        return jnp.where(nt[w] > 0, f, MOE_NF - 1)

    grid_spec = pltpu.PrefetchScalarGridSpec(
        num_scalar_prefetch=4,
        grid=(n_items, MOE_NF),
        in_specs=[
            pl.BlockSpec(memory_space=pl.ANY),
            pl.BlockSpec((None, D_MODEL, MOE_TF), lambda w, f, e, r0, nt, nu: (e[w], 0, fidx(w, f, nt))),
            pl.BlockSpec((None, D_MODEL, MOE_TF),
                         lambda w, f, e, r0, nt, nu: (e[w], 0, MOE_NF + fidx(w, f, nt))),
            pl.BlockSpec((None, 1, MOE_TF), lambda w, f, e, r0, nt, nu: (e[w], 0, fidx(w, f, nt))),
            pl.BlockSpec((None, 1, MOE_TF),
                         lambda w, f, e, r0, nt, nu: (e[w], 0, MOE_NF + fidx(w, f, nt))),
            pl.BlockSpec((None, MOE_TF, D_MODEL), lambda w, f, e, r0, nt, nu: (e[w], fidx(w, f, nt), 0)),
            pl.BlockSpec((None, 1, D_MODEL), lambda w, f, e, r0, nt, nu: (e[w], 0, 0)),
        ],
        out_specs=pl.BlockSpec(memory_space=pl.ANY),
        scratch_shapes=[
            pltpu.VMEM((MOE_ITEM_ROWS, D_MODEL), BF16),
            pltpu.VMEM((MOE_ITEM_ROWS, D_MODEL), F32),
            pltpu.VMEM((MOE_TM * ROW_CHUNKS, LANES), F32),
            pltpu.SemaphoreType.DMA(()),
            pltpu.SemaphoreType.DMA(()),
        ],
    )
    return pl.pallas_call(
        _moe_body,
        grid_spec=grid_spec,
        out_shape=jax.ShapeDtypeStruct((n_rows * ROW_CHUNKS, LANES), F32),
        compiler_params=_params(("arbitrary", "arbitrary")),
        name="moe_ffn",
    )(item_e, item_row0, item_nt, n_used, xs, w_gate_up, w_gate_up,
      b_gate_up.reshape(N_EXPERTS, 1, 2 * D_FF), b_gate_up.reshape(N_EXPERTS, 1, 2 * D_FF),
      w_down, b_down.reshape(N_EXPERTS, 1, D_MODEL))


COMB_TM = 128


def _combine_body(dest_ref, y_hbm, gate_ref, h_ref, g_ref, o_ref, stage, ff, sem):
    def issue(r, carry):
        for k in range(TOP_K):
            _row_copy(y_hbm, stage.at[k], dest_ref[0, r * TOP_K + k], r, sem).start()
        return carry

    lax.fori_loop(0, COMB_TM, issue, 0)

    def drain(r, carry):
        for k in range(TOP_K):
            _row_copy(y_hbm, stage.at[k], dest_ref[0, r * TOP_K + k], r, sem).wait()
        return carry

    lax.fori_loop(0, COMB_TM, drain, 0)

    gates = gate_ref[...]
    for s in range(ROW_CHUNKS):
        acc = gates[:, 0:1] * stage[0, pl.ds(s, COMB_TM, stride=ROW_CHUNKS), :]
        for k in range(1, TOP_K):
            acc = acc + gates[:, k:k + 1] * stage[k, pl.ds(s, COMB_TM, stride=ROW_CHUNKS), :]
        ff[:, s * LANES:(s + 1) * LANES] = acc
    h = h_ref[...] + ff[...]
    inv = lax.rsqrt(jnp.mean(h * h, axis=-1, keepdims=True) + NORM_EPS)
    o_ref[...] = h * inv * g_ref[...]


def _combine(dest, y_chunks, gates, h1, g):
    t_real, d = h1.shape
    n_tiles = t_real // COMB_TM
    return pl.pallas_call(
        _combine_body,
        grid=(n_tiles,),
        in_specs=[
            pl.BlockSpec((None, 1, COMB_TM * TOP_K), lambda i: (i, 0, 0),
                         memory_space=pltpu.SMEM),
            pl.BlockSpec(memory_space=pl.ANY),
            pl.BlockSpec((COMB_TM, LANES), lambda i: (i, 0)),
            pl.BlockSpec((COMB_TM, d), lambda i: (i, 0)),
            pl.BlockSpec((1, d), lambda i: (0, 0)),
        ],
        out_specs=pl.BlockSpec((COMB_TM, d), lambda i: (i, 0)),
        out_shape=jax.ShapeDtypeStruct((t_real, d), F32),
        scratch_shapes=[
            pltpu.VMEM((TOP_K, COMB_TM * ROW_CHUNKS, LANES), F32),
            pltpu.VMEM((COMB_TM, d), F32),
            pltpu.SemaphoreType.DMA(()),
        ],
        compiler_params=_params(("arbitrary",)),
        name="moe_combine",
    )(dest.reshape(n_tiles, 1, COMB_TM * TOP_K), y_chunks, gates, h1, g)


def _routing_plan(top_i, n_tiles, n_items):
    t_real = top_i.shape[0]
    flat_e = top_i.reshape(-1)
    onehot = (flat_e[:, None] == jnp.arange(N_EXPERTS, dtype=jnp.int32)[None, :]).astype(jnp.int32)
    incl = jnp.cumsum(onehot, axis=0)
    rank = jnp.sum((incl - onehot) * onehot, axis=1)
    counts = incl[-1]
    tiles_e = (counts + MOE_TM - 1) // MOE_TM
    pad_end = jnp.cumsum(tiles_e) * MOE_TM
    pad_start = pad_end - tiles_e * MOE_TM
    dest = pad_start[flat_e] + rank
    flat_tok = jnp.arange(t_real * TOP_K, dtype=jnp.int32) // TOP_K
    row_tok = jnp.zeros((n_tiles * MOE_TM,), jnp.int32).at[dest].set(flat_tok)
    n_used = (pad_end[-1] // MOE_TM).astype(jnp.int32).reshape(1)

    items_e = (tiles_e + MOE_ITEM_TILES - 1) // MOE_ITEM_TILES
    item_end = jnp.cumsum(items_e)
    item_start = item_end - items_e
    wid = jnp.arange(n_items, dtype=jnp.int32)
    e_of = jnp.minimum(jnp.searchsorted(item_end, wid, side="right"), N_EXPERTS - 1).astype(jnp.int32)
    k_of = wid - item_start[e_of]
    live = wid < item_end[-1]
    last_e = jnp.max(jnp.where(counts > 0, jnp.arange(N_EXPERTS, dtype=jnp.int32), 0))
    item_e = jnp.where(live, e_of, last_e).astype(jnp.int32)
    item_nt = jnp.where(live, jnp.minimum(MOE_ITEM_TILES, tiles_e[e_of] - k_of * MOE_ITEM_TILES), 0)
    item_row0 = jnp.where(live, pad_start[e_of] + k_of * MOE_ITEM_ROWS, 0)
    return (row_tok, n_used, dest.astype(jnp.int32), item_e,
            item_row0.astype(jnp.int32), item_nt.astype(jnp.int32))


def kernel(x, meta_tokens, norm_mix_g, w_in, b_fgate, b_glu, conv_w, conv_b, gn_g, gn_b, w_out,
           norm_ffn_g, router_w, router_b, w_gate_up, b_gate_up, w_down, b_down, norm_final_g):
    assert x.shape[0] == 1 and norm_mix_g.shape[0] == 1
    xt = x[0]
    t_real = xt.shape[0]
    fcol = 3 * ATT_WIDTH
    w_in0 = w_in[0]
    w_main = jnp.concatenate([w_in0[:, :fcol], w_in0[:, fcol + N_HEADS:]], axis=1).astype(BF16)
    w_f = jnp.pad(w_in0[:, fcol:fcol + N_HEADS], ((0, 0), (0, LANES - N_HEADS))).astype(BF16)
    g_mix = norm_mix_g[0].reshape(1, D_MODEL)

    proj, f_real = _inproj(xt, g_mix, w_main, w_f, tm=1024, tn=512)
    proj_meta, f_meta = _inproj(meta_tokens, g_mix, w_main, w_f, tm=N_META, tn=512)

    b_f = jnp.pad(b_fgate[0], (0, LANES - N_HEADS)).reshape(1, LANES)
    qa, ka, kam = _fgate_bias(f_meta, f_real, b_f)
    v_meta = proj_meta[:, 2 * ATT_WIDTH:3 * ATT_WIDTH]
    vm_t = v_meta.reshape(N_META, N_HEADS, HEAD_DIM).transpose(1, 2, 0)
    att = _attention(proj, proj_meta, vm_t, qa, ka, kam, tq=512)
    conv = _conv_mixer(proj, proj_meta, b_glu[0].reshape(1, -1), conv_w[0],
                       conv_b[0].reshape(1, -1), gn_g[0].reshape(1, -1), gn_b[0].reshape(1, -1),
                       tm=512)

    rw = jnp.pad(router_w[0], ((0, 0), (0, LANES - N_EXPERTS)))
    rb = jnp.pad(router_b[0], (0, LANES - N_EXPERTS), constant_values=NEG_BIG).reshape(1, LANES)
    h1, hn_chunks, top_i, gates = _outproj_router(
        att, conv, xt, w_out[0].astype(BF16), norm_ffn_g[0].reshape(1, D_MODEL), rw, rb, tm=256)

    n_assign = t_real * TOP_K
    n_tiles = (n_assign + N_EXPERTS * (MOE_TM - 1)) // MOE_TM + 1
    n_items = N_EXPERTS + (n_tiles * MOE_TM) // MOE_ITEM_ROWS
    row_tok, n_used, dest, item_e, item_row0, item_nt = _routing_plan(
        top_i[:, :TOP_K], n_tiles, n_items)

    xs = _gather_rows(n_used, row_tok, hn_chunks, n_tiles)
    y_chunks = _moe_ffn(item_e, item_row0, item_nt, n_used, xs, w_gate_up[0], b_gate_up[0],
                        w_down[0], b_down[0], n_items)
    out = _combine(dest, y_chunks, gates, h1, norm_final_g.reshape(1, D_MODEL))
    return out[None]
```

```python
import functools

import jax
import jax.numpy as jnp
import numpy as np
from jax import lax
from jax.experimental import pallas as pl
from jax.experimental.pallas import tpu as pltpu

D_MODEL = 2048
N_META = 16
HEAD_DIM = 128
ATT_WIDTH = 1024
N_HEADS = ATT_WIDTH // HEAD_DIM
CONV_CH = 1024
N_GROUPS = 8
GROUP_CH = CONV_CH // N_GROUPS
CONV_K = 31
N_EXPERTS = 32
TOP_K = 4
D_FF = 2048
SWIGLU_LIMIT = 7.0
SWIGLU_ALPHA = 1.702
NORM_EPS = 1e-5
NEG_BIG = -1e30

LANES = 128
SUBLANES = 8
VMEM_LIMIT = 56 * 1024 * 1024
HALF = D_MODEL // 2

F32 = jnp.float32
BF16 = jnp.bfloat16
U32 = jnp.uint32
HI_MASK = 0xFFFF0000


def _params(sem, vmem=VMEM_LIMIT):
    return pltpu.CompilerParams(dimension_semantics=sem, vmem_limit_bytes=vmem)


def _inproj_body(x_ref, g_ref, w_ref, wf_ref, o_ref, f_ref, hn_ref):
    @pl.when(pl.program_id(1) == 0)
    def _():
        x = x_ref[...]
        inv = lax.rsqrt(jnp.mean(x * x, axis=-1, keepdims=True) + NORM_EPS)
        hn = (x * inv * g_ref[...]).astype(BF16)
        hn_ref[...] = hn
        f_ref[...] = jnp.dot(hn, wf_ref[...], preferred_element_type=F32)

    o_ref[...] = jnp.dot(hn_ref[...], w_ref[...], preferred_element_type=F32).astype(o_ref.dtype)


def _inproj(x, g, w_main, w_f, tm, tn):
    m, d = x.shape
    n = w_main.shape[1]
    return pl.pallas_call(
        _inproj_body,
        grid=(m // tm, n // tn),
        in_specs=[
            pl.BlockSpec((tm, d), lambda i, j: (i, 0)),
            pl.BlockSpec((1, d), lambda i, j: (0, 0)),
            pl.BlockSpec((d, tn), lambda i, j: (0, j)),
            pl.BlockSpec((d, LANES), lambda i, j: (0, 0)),
        ],
        out_specs=[
            pl.BlockSpec((tm, tn), lambda i, j: (i, j)),
            pl.BlockSpec((tm, LANES), lambda i, j: (i, 0)),
        ],
        out_shape=[
            jax.ShapeDtypeStruct((m, n), BF16),
            jax.ShapeDtypeStruct((m, LANES), F32),
        ],
        scratch_shapes=[pltpu.VMEM((tm, d), BF16)],
        compiler_params=_params(("arbitrary", "arbitrary")),
        name="inproj",
    )(x, g, w_main, w_f)


ATT_SCALE = 1.0 / float(np.sqrt(HEAD_DIM))
EXP2_MULT = ATT_SCALE * float(np.log2(np.e))
N_SPLIT = 3
FGATE_TB = 512


def _log_sigmoid(x):
    return jnp.minimum(x, 0.0) - jnp.log1p(jnp.exp(-jnp.abs(x)))


def _prefix_rows(lf):
    n = lf.shape[0]
    row = lax.broadcasted_iota(jnp.int32, (n, n), 0)
    col = lax.broadcasted_iota(jnp.int32, (n, n), 1)
    tri = (col <= row).astype(F32)
    return jnp.dot(tri, lf, preferred_element_type=F32, precision=lax.Precision.HIGHEST)


def _bias_columns(c, head):
    lane = lax.broadcasted_iota(jnp.int32, c.shape, 1)
    x = jnp.sum(jnp.where(lane == head, c, 0.0), axis=-1, keepdims=True) * float(np.sqrt(HEAD_DIM))
    hi = x.astype(BF16).astype(F32)
    r1 = x - hi
    mid = r1.astype(BF16).astype(F32)
    lo = r1 - mid
    part = lane % N_SPLIT
    parts = jnp.where(part == 0, hi, jnp.where(part == 1, mid, lo))
    qa = jnp.where(lane < N_SPLIT, parts, jnp.where(lane < 2 * N_SPLIT, 1.0, 0.0))
    ka = jnp.where(lane < N_SPLIT, 1.0, jnp.where(lane < 2 * N_SPLIT, -parts, 0.0))
    return qa.astype(BF16), ka.astype(BF16)


def _fgate_body(fm_ref, f_ref, b_ref, qa_ref, ka_ref, kam_ref, carry_ref):
    @pl.when(pl.program_id(0) == 0)
    def _():
        cm = _prefix_rows(_log_sigmoid(fm_ref[...] + b_ref[...]))
        carry_ref[...] = cm[N_META - 1:N_META, :]
        for head in range(N_HEADS):
            kam_ref[head] = _bias_columns(cm, head)[1]

    c = _prefix_rows(_log_sigmoid(f_ref[...] + b_ref[...])) + carry_ref[...]
    carry_ref[...] = c[FGATE_TB - 1:FGATE_TB, :]
    for head in range(N_HEADS):
        qa, ka = _bias_columns(c, head)
        qa_ref[head] = qa
        ka_ref[head] = ka


def _fgate_bias(f_meta, f_real, b_f):
    t_real = f_real.shape[0]
    return pl.pallas_call(
        _fgate_body,
        grid=(t_real // FGATE_TB,),
        in_specs=[
            pl.BlockSpec((N_META, LANES), lambda i: (0, 0)),
            pl.BlockSpec((FGATE_TB, LANES), lambda i: (i, 0)),
            pl.BlockSpec((1, LANES), lambda i: (0, 0)),
        ],
        out_specs=[
            pl.BlockSpec((N_HEADS, FGATE_TB, LANES), lambda i: (0, i, 0)),
            pl.BlockSpec((N_HEADS, FGATE_TB, LANES), lambda i: (0, i, 0)),
            pl.BlockSpec((N_HEADS, N_META, LANES), lambda i: (0, 0, 0)),
        ],
        out_shape=[
            jax.ShapeDtypeStruct((N_HEADS, t_real, LANES), BF16),
            jax.ShapeDtypeStruct((N_HEADS, t_real, LANES), BF16),
            jax.ShapeDtypeStruct((N_HEADS, N_META, LANES), BF16),
        ],
        scratch_shapes=[pltpu.VMEM((1, LANES), F32)],
        compiler_params=_params(("arbitrary",)),
        name="fgate_bias",
    )(f_meta, f_real, b_f)


def _attn_body(q_ref, qa_ref, k_ref, ka_ref, v_ref, km_ref, kam_ref, vmt_ref, o_ref, vt_ref, *, tq):
    i = pl.program_id(1)
    n_blocks = vt_ref.shape[0]
    contract_last = (((1,), (1,)), ((), ()))

    @pl.when(i == 0)
    def _():
        for blk in range(n_blocks):
            vt_ref[blk] = v_ref[blk * tq:(blk + 1) * tq, :].T

    q = jnp.concatenate([q_ref[...], qa_ref[...]], axis=1)

    def scores_t(k, ka):
        return lax.dot_general(jnp.concatenate([k, ka], axis=1), q, contract_last,
                               preferred_element_type=F32)

    s = scores_t(km_ref[...], kam_ref[...])
    m = jnp.max(s, axis=0, keepdims=True)
    p = jnp.exp2((s - m) * EXP2_MULT)
    l = jnp.sum(p, axis=0, keepdims=True)
    acc = jnp.dot(vmt_ref[...], p.astype(BF16), preferred_element_type=F32)

    def block_scores(j):
        rows = pl.ds(pl.multiple_of(j * tq, tq), tq)
        return scores_t(k_ref[rows, :], ka_ref[rows, :])

    def absorb(j, s, m_prev, l_prev, acc_prev):
        m_new = jnp.maximum(m_prev, jnp.max(s, axis=0, keepdims=True))
        alpha = jnp.exp2((m_prev - m_new) * EXP2_MULT)
        p = jnp.exp2((s - m_new) * EXP2_MULT)
        l_new = alpha * l_prev + jnp.sum(p, axis=0, keepdims=True)
        acc_new = alpha * acc_prev + jnp.dot(vt_ref[j], p.astype(BF16),
                                             preferred_element_type=F32)
        return m_new, l_new, acc_new

    def step(j, carry):
        s_cur, m_prev, l_prev, acc_prev = carry
        s_next = block_scores(j + 1)
        return (s_next,) + absorb(j, s_cur, m_prev, l_prev, acc_prev)

    s_diag, m, l, acc = lax.fori_loop(0, i, step, (block_scores(0), m, l, acc))
    key = lax.broadcasted_iota(jnp.int32, s_diag.shape, 0)
    qry = lax.broadcasted_iota(jnp.int32, s_diag.shape, 1)
    m, l, acc = absorb(i, jnp.where(key <= qry, s_diag, NEG_BIG), m, l, acc)
    o_ref[...] = (acc / l).T.astype(o_ref.dtype)


def _attention(proj, proj_meta, vm_t, qa, ka, kam, tq):
    t_real = proj.shape[0]
    kcol = ATT_WIDTH // HEAD_DIM
    vcol = 2 * ATT_WIDTH // HEAD_DIM
    return pl.pallas_call(
        functools.partial(_attn_body, tq=tq),
        grid=(N_HEADS, t_real // tq),
        in_specs=[
            pl.BlockSpec((tq, HEAD_DIM), lambda h, i: (i, h)),
            pl.BlockSpec((None, tq, LANES), lambda h, i: (h, i, 0)),
            pl.BlockSpec((t_real, HEAD_DIM), lambda h, i: (0, kcol + h)),
            pl.BlockSpec((None, t_real, LANES), lambda h, i: (h, 0, 0)),
            pl.BlockSpec((t_real, HEAD_DIM), lambda h, i: (0, vcol + h)),
            pl.BlockSpec((N_META, HEAD_DIM), lambda h, i: (0, kcol + h)),
            pl.BlockSpec((None, N_META, LANES), lambda h, i: (h, 0, 0)),
            pl.BlockSpec((None, HEAD_DIM, N_META), lambda h, i: (h, 0, 0)),
        ],
        out_specs=pl.BlockSpec((tq, HEAD_DIM), lambda h, i: (i, h)),
        out_shape=jax.ShapeDtypeStruct((t_real, ATT_WIDTH), BF16),
        scratch_shapes=[pltpu.VMEM((t_real // tq, HEAD_DIM, tq), BF16)],
        compiler_params=_params(("arbitrary", "arbitrary")),
        name="fox_attention",
    )(proj, qa, proj, ka, proj, proj_meta, kam, vm_t)


CONV_HIST = 32
CONV_ROWS = 128


def _glu(a_ref, g_ref, bglu_ref):
    a = a_ref[...].astype(F32) + bglu_ref[:, :CONV_CH]
    g = g_ref[...].astype(F32) + bglu_ref[:, CONV_CH:]
    return a * jax.nn.sigmoid(g)


def _conv_body(a_ref, g_ref, am_ref, gm_ref, bglu_ref, w_ref, cb_ref, gng_ref, gnb_ref,
               o_ref, ubuf, *, tm):
    i = pl.program_id(0)

    @pl.when(i == 0)
    def _():
        ubuf[0:CONV_HIST - N_META, :] = jnp.zeros((CONV_HIST - N_META, CONV_CH), F32)
        ubuf[CONV_HIST - N_META:CONV_HIST, :] = _glu(am_ref, gm_ref, bglu_ref)

    @pl.when(i > 0)
    def _():
        ubuf[0:CONV_HIST, :] = ubuf[tm:tm + CONV_HIST, :]

    ubuf[CONV_HIST:CONV_HIST + tm, :] = _glu(a_ref, g_ref, bglu_ref)

    first = CONV_HIST - (CONV_K - 1)
    for grp in range(N_GROUPS):
        lanes = slice(grp * GROUP_CH, (grp + 1) * GROUP_CH)
        for rc in range(tm // CONV_ROWS):
            base = rc * CONV_ROWS
            acc = jnp.zeros((CONV_ROWS, GROUP_CH), F32)
            for tap in range(CONV_K):
                acc = acc + ubuf[base + first + tap:base + first + tap + CONV_ROWS, lanes] * \
                    w_ref[tap:tap + 1, lanes]
            acc = acc + cb_ref[:, lanes]
            mu = jnp.mean(acc, axis=-1, keepdims=True)
            dlt = acc - mu
            var = jnp.mean(dlt * dlt, axis=-1, keepdims=True)
            y = dlt * lax.rsqrt(var + NORM_EPS) * gng_ref[:, lanes] + gnb_ref[:, lanes]
            o_ref[base:base + CONV_ROWS, lanes] = (y * jax.nn.sigmoid(y)).astype(o_ref.dtype)


def _conv_mixer(proj, proj_meta, b_glu, conv_w, conv_b, gn_g, gn_b, tm):
    t_real = proj.shape[0]
    acol = 3 * ATT_WIDTH // CONV_CH
    gcol = acol + 1
    const = lambda i: (0, 0)
    return pl.pallas_call(
        functools.partial(_conv_body, tm=tm),
        grid=(t_real // tm,),
        in_specs=[
            pl.BlockSpec((tm, CONV_CH), lambda i: (i, acol)),
            pl.BlockSpec((tm, CONV_CH), lambda i: (i, gcol)),
            pl.BlockSpec((N_META, CONV_CH), lambda i: (0, acol)),
            pl.BlockSpec((N_META, CONV_CH), lambda i: (0, gcol)),
            pl.BlockSpec((1, 2 * CONV_CH), const),
            pl.BlockSpec((CONV_K, CONV_CH), const),
            pl.BlockSpec((1, CONV_CH), const),
            pl.BlockSpec((1, CONV_CH), const),
            pl.BlockSpec((1, CONV_CH), const),
        ],
        out_specs=pl.BlockSpec((tm, CONV_CH), lambda i: (i, 0)),
        out_shape=jax.ShapeDtypeStruct((t_real, CONV_CH), BF16),
        scratch_shapes=[pltpu.VMEM((tm + CONV_HIST, CONV_CH), F32)],
        compiler_params=_params(("arbitrary",)),
        name="conv_mixer",
    )(proj, proj, proj_meta, proj_meta, b_glu, conv_w, conv_b, gn_g, gn_b)


def _pack_bf16_pairs(hi_half, lo_half):
    lo = lax.bitcast_convert_type(lo_half.astype(F32), U32) >> 16
    hi = lax.bitcast_convert_type(hi_half.astype(F32), U32) & jnp.uint32(HI_MASK)
    return lo | hi


def _unpack_bf16_pairs(words):
    lo = lax.bitcast_convert_type(words << 16, F32).astype(BF16)
    hi = lax.bitcast_convert_type(words & jnp.uint32(HI_MASK), F32).astype(BF16)
    return jnp.concatenate([lo, hi], axis=1)


def _outproj_body(att_ref, conv_ref, x_ref, w_ref, g_ref, rwh_ref, rwl_ref, rb_ref,
                  h_ref, hnp_ref, topi_ref, gate_ref):
    mix = jnp.dot(att_ref[...], w_ref[0:ATT_WIDTH, :], preferred_element_type=F32)
    mix = mix + jnp.dot(conv_ref[...], w_ref[ATT_WIDTH:, :], preferred_element_type=F32)
    h = x_ref[...] + mix
    h_ref[...] = h
    inv = lax.rsqrt(jnp.mean(h * h, axis=-1, keepdims=True) + NORM_EPS)
    hn = h * inv * g_ref[...]
    hn_hi = hn.astype(BF16)
    hnp_ref[...] = _pack_bf16_pairs(hn_hi[:, HALF:], hn_hi[:, :HALF])

    hn_lo = (hn - hn_hi.astype(F32)).astype(BF16)
    logits = jnp.dot(hn_hi, rwh_ref[...], preferred_element_type=F32)
    logits = logits + jnp.dot(hn_hi, rwl_ref[...], preferred_element_type=F32)
    logits = logits + jnp.dot(hn_lo, rwh_ref[...], preferred_element_type=F32) + rb_ref[...]
    lane = lax.broadcasted_iota(jnp.int32, logits.shape, 1)
    vals = logits
    tops = []
    idxs = []
    for _ in range(TOP_K):
        mx = jnp.max(vals, axis=-1, keepdims=True)
        ix = jnp.min(jnp.where(vals == mx, lane, LANES), axis=-1, keepdims=True)
        tops.append(mx)
        idxs.append(ix)
        vals = jnp.where(lane == ix, -jnp.inf, vals)
    exps = [jnp.exp(v - tops[0]) for v in tops]
    denom = exps[0] + exps[1] + exps[2] + exps[3]
    topi = jnp.full(logits.shape, -1, jnp.int32)
    gate = jnp.zeros(logits.shape, F32)
    for k in range(TOP_K):
        topi = jnp.where(lane == k, idxs[k], topi)
        gate = jnp.where(lane == k, exps[k] / denom, gate)
    topi_ref[...] = topi
    gate_ref[...] = gate


def _outproj_router(att, conv, x, w_out, g, rw_hi, rw_lo, router_b, tm):
    t_real, d = x.shape
    const = lambda i: (0, 0)
    return pl.pallas_call(
        _outproj_body,
        grid=(t_real // tm,),
        in_specs=[
            pl.BlockSpec((tm, ATT_WIDTH), lambda i: (i, 0)),
            pl.BlockSpec((tm, CONV_CH), lambda i: (i, 0)),
            pl.BlockSpec((tm, d), lambda i: (i, 0)),
            pl.BlockSpec((d, d), const),
            pl.BlockSpec((1, d), const),
            pl.BlockSpec((d, LANES), const),
            pl.BlockSpec((d, LANES), const),
            pl.BlockSpec((1, LANES), const),
        ],
        out_specs=[
            pl.BlockSpec((tm, d), lambda i: (i, 0)),
            pl.BlockSpec((tm, HALF), lambda i: (i, 0)),
            pl.BlockSpec((tm, LANES), lambda i: (i, 0)),
            pl.BlockSpec((tm, LANES), lambda i: (i, 0)),
        ],
        out_shape=[
            jax.ShapeDtypeStruct((t_real, d), F32),
            jax.ShapeDtypeStruct((t_real, HALF), U32),
            jax.ShapeDtypeStruct((t_real, LANES), jnp.int32),
            jax.ShapeDtypeStruct((t_real, LANES), F32),
        ],
        compiler_params=_params(("arbitrary",)),
        name="outproj_router",
    )(att, conv, x, w_out, g, rw_hi, rw_lo, router_b)


MOE_TM = 256
PLAN_TB = 256
PLAN_ROWS = SUBLANES


def _plan_body(topi_ref, dest_ref, meta_ref, cnt_ref, carry_ref, start_ref):
    phase = pl.program_id(0)
    j = pl.program_id(1)
    topi = topi_ref[...]
    lane = lax.broadcasted_iota(jnp.int32, topi.shape, 1)
    picks = [topi[:, k:k + 1] for k in range(TOP_K)]
    onehot = jnp.zeros(topi.shape, F32)
    for pick in picks:
        onehot = onehot + (lane == pick).astype(F32)
    colsum = jnp.sum(onehot, axis=0, keepdims=True)

    @pl.when((phase == 0) & (j == 0))
    def _():
        cnt_ref[...] = jnp.zeros(cnt_ref.shape, F32)

    @pl.when(phase == 0)
    def _():
        cnt_ref[...] += colsum

    @pl.when((phase == 1) & (j == 0))
    def _():
        cnt = jnp.broadcast_to(cnt_ref[...], (SUBLANES, LANES))
        tiles = jnp.floor((cnt + (MOE_TM - 1)) * (1.0 / MOE_TM))
        padded = tiles * MOE_TM
        row = lax.broadcasted_iota(jnp.int32, (LANES, LANES), 0)
        col = lax.broadcasted_iota(jnp.int32, (LANES, LANES), 1)
        upper = (row <= col).astype(F32)
        pad_end = jnp.dot(padded, upper, preferred_element_type=F32,
                          precision=lax.Precision.HIGHEST)
        start = pad_end - padded
        start_ref[...] = start[0:1, :]
        carry_ref[...] = jnp.zeros(carry_ref.shape, F32)
        sub = lax.broadcasted_iota(jnp.int32, (PLAN_ROWS, LANES), 0)
        meta_ref[...] = jnp.where(sub == 0, cnt, jnp.where(sub == 1, start, tiles)).astype(jnp.int32)

    @pl.when(phase == 1)
    def _():
        row = lax.broadcasted_iota(jnp.int32, (PLAN_TB, PLAN_TB), 0)
        col = lax.broadcasted_iota(jnp.int32, (PLAN_TB, PLAN_TB), 1)
        earlier = (col < row).astype(BF16)
        before = jnp.dot(earlier, onehot.astype(BF16), preferred_element_type=F32)
        slot = before + carry_ref[...] + start_ref[...]
        dest = jnp.zeros(topi.shape, jnp.int32)
        for k, pick in enumerate(picks):
            d_k = jnp.sum(jnp.where(lane == pick, slot, 0.0), axis=-1, keepdims=True)
            dest = jnp.where(lane == k, d_k.astype(jnp.int32), dest)
        dest_ref[...] = dest
        carry_ref[...] += colsum


def _routing_plan(top_i):
    t_real = top_i.shape[0]
    return pl.pallas_call(
        _plan_body,
        grid=(2, t_real // PLAN_TB),
        in_specs=[pl.BlockSpec((PLAN_TB, LANES), lambda p, j: (j, 0))],
        out_specs=[
            pl.BlockSpec((PLAN_TB, LANES), lambda p, j: (j * p, 0)),
            pl.BlockSpec((PLAN_ROWS, LANES), lambda p, j: (0, 0)),
        ],
        out_shape=[
            jax.ShapeDtypeStruct((t_real, LANES), jnp.int32),
            jax.ShapeDtypeStruct((PLAN_ROWS, LANES), jnp.int32),
        ],
        scratch_shapes=[pltpu.VMEM((1, LANES), F32)] * 3,
        compiler_params=_params(("arbitrary", "arbitrary")),
        name="routing_plan",
    )(top_i)


MOE_ITEM_TILES = 5
MOE_ITEM_ROWS = MOE_ITEM_TILES * MOE_TM


def _item_tables(counts, starts, tiles_e, n_items):
    items_e = (tiles_e + MOE_ITEM_TILES - 1) // MOE_ITEM_TILES
    item_end = jnp.cumsum(items_e)
    item_start = item_end - items_e
    wid = jnp.arange(n_items, dtype=jnp.int32)
    e_of = jnp.minimum(jnp.sum((item_end[None, :] <= wid[:, None]).astype(jnp.int32), axis=1),
                       N_EXPERTS - 1)
    k_of = wid - item_start[e_of]
    live = wid < item_end[-1]
    last_e = jnp.max(jnp.where(counts > 0, jnp.arange(N_EXPERTS, dtype=jnp.int32), 0))
    item_e = jnp.where(live, e_of, last_e).astype(jnp.int32)
    item_nt = jnp.where(live, jnp.minimum(MOE_ITEM_TILES, tiles_e[e_of] - k_of * MOE_ITEM_TILES), 0)
    item_row0 = jnp.where(live, starts[e_of] + k_of * MOE_ITEM_ROWS, 0)
    return item_e, item_row0.astype(jnp.int32), item_nt.astype(jnp.int32)


DISP_TB = 256


def _dispatch_body(cnt_ref, start_ref, tiles_ref, nused_ref, dest_ref, src_hbm, xs_hbm,
                   zrow, ztile, sem, zsem):
    t = pl.program_id(0)
    n_steps = pl.num_programs(0)
    n_tiles = xs_hbm.shape[0] // MOE_TM
    pushes = DISP_TB * TOP_K

    def push(r, k):
        return pltpu.make_async_copy(
            src_hbm.at[pl.ds(t * DISP_TB + r, 1), :],
            xs_hbm.at[pl.ds(dest_ref[0, r * TOP_K + k], 1), :], sem)

    def issue(r, carry):
        for k in range(TOP_K):
            push(r, k).start(priority=k % 2)
        return carry

    lax.fori_loop(0, DISP_TB, issue, 0)

    def drain(r, carry):
        for k in range(TOP_K):
            push(r, k).wait()
        return carry

    @pl.when(t > 0)
    def _():
        lax.fori_loop(0, DISP_TB, drain, 0)

    @pl.when(t == n_steps - 1)
    def _():
        lax.fori_loop(0, DISP_TB, drain, 0)
        zrow[...] = jnp.zeros(zrow.shape, zrow.dtype)
        ztile[...] = jnp.zeros(ztile.shape, ztile.dtype)

        def zero_row(r):
            return pltpu.make_async_copy(zrow, xs_hbm.at[pl.ds(r, 1), :], zsem)

        def per_expert(e, carry):
            lo = start_ref[e] + cnt_ref[e]
            hi = start_ref[e] + tiles_ref[e] * MOE_TM
            lax.fori_loop(lo, hi, lambda r, c: (zero_row(r).start(), c)[1], 0)
            lax.fori_loop(lo, hi, lambda r, c: (zero_row(r).wait(), c)[1], 0)
            return carry

        lax.fori_loop(0, N_EXPERTS, per_expert, 0)

        def zero_tile(tix, carry):
            cp = pltpu.make_async_copy(
                ztile, xs_hbm.at[pl.ds(pl.multiple_of(tix * MOE_TM, MOE_TM), MOE_TM), :], zsem)
            cp.start()
            cp.wait()
            return carry

        lax.fori_loop(nused_ref[0], n_tiles, zero_tile, 0)


def _dispatch(counts, starts, tiles_e, n_used, dest4, hn_packed, n_tiles):
    t_real = hn_packed.shape[0]
    n_steps = t_real // DISP_TB
    grid_spec = pltpu.PrefetchScalarGridSpec(
        num_scalar_prefetch=4,
        grid=(n_steps,),
        in_specs=[
            pl.BlockSpec((None, 1, DISP_TB * TOP_K), lambda t, *_: (t, 0, 0),
                         memory_space=pltpu.SMEM),
            pl.BlockSpec(memory_space=pl.ANY),
        ],
        out_specs=pl.BlockSpec(memory_space=pl.ANY),
        scratch_shapes=[
            pltpu.VMEM((1, HALF), U32),
            pltpu.VMEM((MOE_TM, HALF), U32),
            pltpu.SemaphoreType.DMA(()),
            pltpu.SemaphoreType.DMA(()),
        ],
    )
    return pl.pallas_call(
        _dispatch_body,
        grid_spec=grid_spec,
        out_shape=jax.ShapeDtypeStruct((n_tiles * MOE_TM, HALF), U32),
        compiler_params=_params(("arbitrary",)),
        name="moe_dispatch",
    )(counts, starts, tiles_e, n_used, dest4.reshape(n_steps, 1, DISP_TB * TOP_K), hn_packed)


MOE_TF = 512
MOE_NF = D_FF // MOE_TF


def _moe_body(e_ref, row0_ref, nt_ref, nused_ref, xs_hbm, wg_ref, wu_ref, bg_ref, bu_ref,
              wd_ref, bd_ref, y_hbm, xq, yacc, sem_x, sem_y):
    w = pl.program_id(0)
    f = pl.program_id(1)
    n_items = pl.num_programs(0)
    nt = nt_ref[w]
    slot = w % 2

    def x_copy(item, r, buf):
        return pltpu.make_async_copy(
            xs_hbm.at[pl.ds(pl.multiple_of(row0_ref[item] + r * MOE_TM, MOE_TM), MOE_TM), :],
            xq.at[buf, pl.ds(r * MOE_TM, MOE_TM), :], sem_x.at[buf])

    def y_copy(r):
        return pltpu.make_async_copy(
            yacc.at[pl.ds(pl.multiple_of(r * MOE_TM, MOE_TM), MOE_TM), :],
            y_hbm.at[pl.ds(pl.multiple_of(row0_ref[w] + r * MOE_TM, MOE_TM), MOE_TM), :], sem_y)

    def for_tiles(item, fn):
        for r in range(MOE_ITEM_TILES):
            @pl.when(r < nt_ref[item])
            def _():
                fn(r)

    @pl.when(f == 0)
    def _():
        @pl.when(w == 0)
        def _():
            for_tiles(0, lambda r: x_copy(0, r, 0).start())

        for_tiles(w, lambda r: x_copy(w, r, slot).wait())

        @pl.when(w + 1 < n_items)
        def _():
            nxt = jnp.minimum(w + 1, n_items - 1)
            for_tiles(nxt, lambda r: x_copy(nxt, r, 1 - slot).start())

    @pl.when(nt > 0)
    def _():
        def tile_rows(r):
            return pl.ds(pl.multiple_of(r * MOE_TM, MOE_TM), MOE_TM)

        def hidden(r):
            x = _unpack_bf16_pairs(xq[slot, tile_rows(r), :])
            gate = jnp.dot(x, wg_ref[...].astype(BF16), preferred_element_type=F32) + bg_ref[...]
            up = jnp.dot(x, wu_ref[...].astype(BF16), preferred_element_type=F32) + bu_ref[...]
            gate = jnp.minimum(gate, SWIGLU_LIMIT)
            up = jnp.clip(up, -SWIGLU_LIMIT, SWIGLU_LIMIT)
            return (gate * jax.nn.sigmoid(SWIGLU_ALPHA * gate) * (up + 1.0)).astype(BF16)

        def project(r, act):
            yacc[tile_rows(r), :] += jnp.dot(act, wd_ref[...].astype(BF16),
                                             preferred_element_type=F32)

            @pl.when(f == MOE_NF - 1)
            def _():
                y_copy(r).start()

        def tile(r, act_prev):
            act = hidden(r)
            project(r - 1, act_prev)
            return act

        @pl.when(f == 0)
        def _():
            def seed(r, carry):
                yacc[tile_rows(r), :] = jnp.broadcast_to(bd_ref[...], (MOE_TM, D_MODEL))
                return carry

            lax.fori_loop(0, nt, seed, 0)

        project(nt - 1, lax.fori_loop(1, nt, tile, hidden(0)))

        @pl.when(f == MOE_NF - 1)
        def _():
            for_tiles(w, lambda r: y_copy(r).wait())

    @pl.when((w == n_items - 1) & (f == MOE_NF - 1))
    def _():
        yacc[0:MOE_TM, :] = jnp.zeros((MOE_TM, D_MODEL), F32)

        def zero_tile(tix, carry):
            cp = pltpu.make_async_copy(
                yacc.at[pl.ds(0, MOE_TM), :],
                y_hbm.at[pl.ds(pl.multiple_of(tix * MOE_TM, MOE_TM), MOE_TM), :], sem_y)
            cp.start()
            cp.wait()
            return carry

        lax.fori_loop(nused_ref[0], y_hbm.shape[0] // MOE_TM, zero_tile, 0)


def _moe_ffn(item_e, item_row0, item_nt, n_used, xs, w_gate_up, b_gate_up, w_down, b_down, n_items):
    n_rows = xs.shape[0]

    def fidx(w, f, nt):
        return jnp.where(nt[w] > 0, f, MOE_NF - 1)

    grid_spec = pltpu.PrefetchScalarGridSpec(
        num_scalar_prefetch=4,
        grid=(n_items, MOE_NF),
        in_specs=[
            pl.BlockSpec(memory_space=pl.ANY),
            pl.BlockSpec((None, D_MODEL, MOE_TF),
                         lambda w, f, e, r0, nt, nu: (e[w], 0, fidx(w, f, nt))),
            pl.BlockSpec((None, D_MODEL, MOE_TF),
                         lambda w, f, e, r0, nt, nu: (e[w], 0, MOE_NF + fidx(w, f, nt))),
            pl.BlockSpec((None, 1, MOE_TF), lambda w, f, e, r0, nt, nu: (e[w], 0, fidx(w, f, nt))),
            pl.BlockSpec((None, 1, MOE_TF),
                         lambda w, f, e, r0, nt, nu: (e[w], 0, MOE_NF + fidx(w, f, nt))),
            pl.BlockSpec((None, MOE_TF, D_MODEL),
                         lambda w, f, e, r0, nt, nu: (e[w], fidx(w, f, nt), 0)),
            pl.BlockSpec((None, 1, D_MODEL), lambda w, f, e, r0, nt, nu: (e[w], 0, 0)),
        ],
        out_specs=pl.BlockSpec(memory_space=pl.ANY),
        scratch_shapes=[
            pltpu.VMEM((2, MOE_ITEM_ROWS, HALF), U32),
            pltpu.VMEM((MOE_ITEM_ROWS, D_MODEL), F32),
            pltpu.SemaphoreType.DMA((2,)),
            pltpu.SemaphoreType.DMA(()),
        ],
    )
    return pl.pallas_call(
        _moe_body,
        grid_spec=grid_spec,
        out_shape=jax.ShapeDtypeStruct((n_rows, D_MODEL), F32),
        compiler_params=_params(("arbitrary", "arbitrary")),
        name="moe_ffn",
    )(item_e, item_row0, item_nt, n_used, xs, w_gate_up, w_gate_up,
      b_gate_up.reshape(N_EXPERTS, 1, 2 * D_FF), b_gate_up.reshape(N_EXPERTS, 1, 2 * D_FF),
      w_down, b_down.reshape(N_EXPERTS, 1, D_MODEL))


COMB_TM = 128


def _combine_body(dest_ref, dnext_ref, y_hbm, gate_ref, h_ref, g_ref, o_ref, stage, sem):
    t = pl.program_id(0)
    n_steps = pl.num_programs(0)
    slot = t % 2

    def pull(idx_ref, r, k, buf):
        return pltpu.make_async_copy(
            y_hbm.at[pl.ds(idx_ref[0, r * TOP_K + k], 1), :],
            stage.at[buf, pl.ds(k * COMB_TM + r, 1), :], sem.at[buf])

    def issue_all(idx_ref, buf):
        def issue(r, carry):
            for k in range(TOP_K):
                pull(idx_ref, r, k, buf).start(priority=k % 2)
            return carry

        lax.fori_loop(0, COMB_TM, issue, 0)

    @pl.when(t == 0)
    def _():
        issue_all(dest_ref, 0)

    @pl.when(t + 1 < n_steps)
    def _():
        issue_all(dnext_ref, 1 - slot)

    def drain(r, carry):
        for k in range(TOP_K):
            pull(dest_ref, r, k, slot).wait()
        return carry

    lax.fori_loop(0, COMB_TM, drain, 0)

    gates = gate_ref[...]
    ff = gates[:, 0:1] * stage[slot, 0:COMB_TM, :]
    for k in range(1, TOP_K):
        ff = ff + gates[:, k:k + 1] * stage[slot, k * COMB_TM:(k + 1) * COMB_TM, :]
    h = h_ref[...] + ff
    inv = lax.rsqrt(jnp.mean(h * h, axis=-1, keepdims=True) + NORM_EPS)
    o_ref[...] = h * inv * g_ref[...]


def _combine(dest4, y_rows, gates, h1, g):
    t_real, d = h1.shape
    n_steps = t_real // COMB_TM
    dest_blocks = dest4.reshape(n_steps, 1, COMB_TM * TOP_K)
    smem_block = (None, 1, COMB_TM * TOP_K)
    return pl.pallas_call(
        _combine_body,
        grid=(n_steps,),
        in_specs=[
            pl.BlockSpec(smem_block, lambda i: (i, 0, 0), memory_space=pltpu.SMEM),
            pl.BlockSpec(smem_block, lambda i: (jnp.minimum(i + 1, n_steps - 1), 0, 0),
                         memory_space=pltpu.SMEM),
            pl.BlockSpec(memory_space=pl.ANY),
            pl.BlockSpec((COMB_TM, LANES), lambda i: (i, 0)),
            pl.BlockSpec((COMB_TM, d), lambda i: (i, 0)),
            pl.BlockSpec((1, d), lambda i: (0, 0)),
        ],
        out_specs=pl.BlockSpec((COMB_TM, d), lambda i: (i, 0)),
        out_shape=jax.ShapeDtypeStruct((t_real, d), F32),
        scratch_shapes=[
            pltpu.VMEM((2, TOP_K * COMB_TM, d), F32),
            pltpu.SemaphoreType.DMA((2,)),
        ],
        compiler_params=_params(("arbitrary",)),
        name="moe_combine",
    )(dest_blocks, dest_blocks, y_rows, gates, h1, g)


def kernel(x, meta_tokens, norm_mix_g, w_in, b_fgate, b_glu, conv_w, conv_b, gn_g, gn_b, w_out,
           norm_ffn_g, router_w, router_b, w_gate_up, b_gate_up, w_down, b_down, norm_final_g):
    assert x.shape[0] == 1 and norm_mix_g.shape[0] == 1
    xt = x[0]
    t_real = xt.shape[0]
    fcol = 3 * ATT_WIDTH
    w_in0 = w_in[0]
    w_main = jnp.concatenate([w_in0[:, :fcol], w_in0[:, fcol + N_HEADS:]], axis=1).astype(BF16)
    w_f = jnp.pad(w_in0[:, fcol:fcol + N_HEADS], ((0, 0), (0, LANES - N_HEADS))).astype(BF16)
    g_mix = norm_mix_g[0].reshape(1, D_MODEL)

    proj, f_real = _inproj(xt, g_mix, w_main, w_f, tm=1024, tn=512)
    proj_meta, f_meta = _inproj(meta_tokens, g_mix, w_main, w_f, tm=N_META, tn=512)

    b_f = jnp.pad(b_fgate[0], (0, LANES - N_HEADS)).reshape(1, LANES)
    qa, ka, kam = _fgate_bias(f_meta, f_real, b_f)
    v_meta = proj_meta[:, 2 * ATT_WIDTH:3 * ATT_WIDTH]
    vm_t = v_meta.reshape(N_META, N_HEADS, HEAD_DIM).transpose(1, 2, 0)
    att = _attention(proj, proj_meta, vm_t, qa, ka, kam, tq=512)
    conv = _conv_mixer(proj, proj_meta, b_glu[0].reshape(1, -1), conv_w[0],
                       conv_b[0].reshape(1, -1), gn_g[0].reshape(1, -1), gn_b[0].reshape(1, -1),
                       tm=512)

    rw = jnp.pad(router_w[0], ((0, 0), (0, LANES - N_EXPERTS)))
    rw_hi = rw.astype(BF16)
    rw_lo = (rw - rw_hi.astype(F32)).astype(BF16)
    rb = jnp.pad(router_b[0], (0, LANES - N_EXPERTS), constant_values=NEG_BIG).reshape(1, LANES)
    h1, hn_packed, top_i, gates = _outproj_router(
        att, conv, xt, w_out[0].astype(BF16), norm_ffn_g[0].reshape(1, D_MODEL),
        rw_hi, rw_lo, rb, tm=256)

    n_assign = t_real * TOP_K
    n_tiles = (n_assign + N_EXPERTS * (MOE_TM - 1)) // MOE_TM + 1
    n_items = N_EXPERTS + (n_tiles * MOE_TM) // MOE_ITEM_ROWS
    dest, plan = _routing_plan(top_i)
    counts, starts, tiles_e = (plan[r, :N_EXPERTS] for r in range(3))
    n_used = jnp.sum(tiles_e).astype(jnp.int32).reshape(1)
    item_e, item_row0, item_nt = _item_tables(counts, starts, tiles_e, n_items)
    dest4 = dest[:, :TOP_K]

    xs = _dispatch(counts, starts, tiles_e, n_used, dest4, hn_packed, n_tiles)
    y_rows = _moe_ffn(item_e, item_row0, item_nt, n_used, xs, w_gate_up[0], b_gate_up[0],
                      w_down[0], b_down[0], n_items)
    out = _combine(dest4, y_rows, gates, h1, norm_final_g.reshape(1, D_MODEL))
    return out[None]
```

```python
import functools

import jax
import jax.numpy as jnp
import numpy as np
from jax import lax
from jax.experimental import pallas as pl
from jax.experimental.pallas import tpu as pltpu

D_MODEL = 2048
N_META = 16
HEAD_DIM = 128
ATT_WIDTH = 1024
N_HEADS = ATT_WIDTH // HEAD_DIM
CONV_CH = 1024
N_GROUPS = 8
GROUP_CH = CONV_CH // N_GROUPS
CONV_K = 31
N_EXPERTS = 32
TOP_K = 4
D_FF = 2048
SWIGLU_LIMIT = 7.0
SWIGLU_ALPHA = 1.702
NORM_EPS = 1e-5
NEG_BIG = -1e30

LANES = 128
SUBLANES = 8
VMEM_LIMIT = 56 * 1024 * 1024
HALF = D_MODEL // 2

F32 = jnp.float32
BF16 = jnp.bfloat16
U32 = jnp.uint32
HI_MASK = 0xFFFF0000


def _params(sem, vmem=VMEM_LIMIT):
    return pltpu.CompilerParams(dimension_semantics=sem, vmem_limit_bytes=vmem)


def _inproj_body(x_ref, g_ref, w_ref, wf_ref, o_ref, f_ref, hn_ref):
    @pl.when(pl.program_id(1) == 0)
    def _():
        x = x_ref[...]
        inv = lax.rsqrt(jnp.mean(x * x, axis=-1, keepdims=True) + NORM_EPS)
        hn = (x * inv * g_ref[...]).astype(BF16)
        hn_ref[...] = hn
        f_ref[...] = jnp.dot(hn, wf_ref[...], preferred_element_type=F32)

    o_ref[...] = jnp.dot(hn_ref[...], w_ref[...], preferred_element_type=F32).astype(o_ref.dtype)


def _inproj(x, g, w_main, w_f, tm, tn):
    m, d = x.shape
    n = w_main.shape[1]
    return pl.pallas_call(
        _inproj_body,
        grid=(m // tm, n // tn),
        in_specs=[
            pl.BlockSpec((tm, d), lambda i, j: (i, 0)),
            pl.BlockSpec((1, d), lambda i, j: (0, 0)),
            pl.BlockSpec((d, tn), lambda i, j: (0, j)),
            pl.BlockSpec((d, LANES), lambda i, j: (0, 0)),
        ],
        out_specs=[
            pl.BlockSpec((tm, tn), lambda i, j: (i, j)),
            pl.BlockSpec((tm, LANES), lambda i, j: (i, 0)),
        ],
        out_shape=[
            jax.ShapeDtypeStruct((m, n), BF16),
            jax.ShapeDtypeStruct((m, LANES), F32),
        ],
        scratch_shapes=[pltpu.VMEM((tm, d), BF16)],
        compiler_params=_params(("arbitrary", "arbitrary")),
        name="inproj",
    )(x, g, w_main, w_f)


ATT_SCALE = 1.0 / float(np.sqrt(HEAD_DIM))
EXP2_MULT = ATT_SCALE * float(np.log2(np.e))
N_SPLIT = 3
FGATE_TB = 512


def _log_sigmoid(x):
    return jnp.minimum(x, 0.0) - jnp.log1p(jnp.exp(-jnp.abs(x)))


def _prefix_rows(lf):
    n = lf.shape[0]
    row = lax.broadcasted_iota(jnp.int32, (n, n), 0)
    col = lax.broadcasted_iota(jnp.int32, (n, n), 1)
    tri = (col <= row).astype(F32)
    return jnp.dot(tri, lf, preferred_element_type=F32, precision=lax.Precision.HIGHEST)


def _bias_columns(c, head):
    lane = lax.broadcasted_iota(jnp.int32, c.shape, 1)
    x = jnp.sum(jnp.where(lane == head, c, 0.0), axis=-1, keepdims=True) * float(np.sqrt(HEAD_DIM))
    hi = x.astype(BF16).astype(F32)
    r1 = x - hi
    mid = r1.astype(BF16).astype(F32)
    lo = r1 - mid
    part = lane % N_SPLIT
    parts = jnp.where(part == 0, hi, jnp.where(part == 1, mid, lo))
    qa = jnp.where(lane < N_SPLIT, parts, jnp.where(lane < 2 * N_SPLIT, 1.0, 0.0))
    ka = jnp.where(lane < N_SPLIT, 1.0, jnp.where(lane < 2 * N_SPLIT, -parts, 0.0))
    return qa.astype(BF16), ka.astype(BF16)


def _fgate_body(fm_ref, f_ref, b_ref, qa_ref, ka_ref, kam_ref, carry_ref):
    @pl.when(pl.program_id(0) == 0)
    def _():
        cm = _prefix_rows(_log_sigmoid(fm_ref[...] + b_ref[...]))
        carry_ref[...] = cm[N_META - 1:N_META, :]
        for head in range(N_HEADS):
            kam_ref[head] = _bias_columns(cm, head)[1]

    c = _prefix_rows(_log_sigmoid(f_ref[...] + b_ref[...])) + carry_ref[...]
    carry_ref[...] = c[FGATE_TB - 1:FGATE_TB, :]
    for head in range(N_HEADS):
        qa, ka = _bias_columns(c, head)
        qa_ref[head] = qa
        ka_ref[head] = ka


def _fgate_bias(f_meta, f_real, b_f):
    t_real = f_real.shape[0]
    return pl.pallas_call(
        _fgate_body,
        grid=(t_real // FGATE_TB,),
        in_specs=[
            pl.BlockSpec((N_META, LANES), lambda i: (0, 0)),
            pl.BlockSpec((FGATE_TB, LANES), lambda i: (i, 0)),
            pl.BlockSpec((1, LANES), lambda i: (0, 0)),
        ],
        out_specs=[
            pl.BlockSpec((N_HEADS, FGATE_TB, LANES), lambda i: (0, i, 0)),
            pl.BlockSpec((N_HEADS, FGATE_TB, LANES), lambda i: (0, i, 0)),
            pl.BlockSpec((N_HEADS, N_META, LANES), lambda i: (0, 0, 0)),
        ],
        out_shape=[
            jax.ShapeDtypeStruct((N_HEADS, t_real, LANES), BF16),
            jax.ShapeDtypeStruct((N_HEADS, t_real, LANES), BF16),
            jax.ShapeDtypeStruct((N_HEADS, N_META, LANES), BF16),
        ],
        scratch_shapes=[pltpu.VMEM((1, LANES), F32)],
        compiler_params=_params(("arbitrary",)),
        name="fgate_bias",
    )(f_meta, f_real, b_f)


def _attn_body(q_ref, qa_ref, k_ref, ka_ref, v_ref, km_ref, kam_ref, vmt_ref, o_ref, vt_ref, *, tq):
    i = pl.program_id(1)
    n_blocks, _, tk = vt_ref.shape
    diag_blocks = tq // tk
    contract_last = (((1,), (1,)), ((), ()))

    @pl.when(i == 0)
    def _():
        for blk in range(n_blocks):
            vt_ref[blk] = v_ref[blk * tk:(blk + 1) * tk, :].T

    q = jnp.concatenate([q_ref[...], qa_ref[...]], axis=1)

    def scores_t(k, ka):
        return lax.dot_general(jnp.concatenate([k, ka], axis=1), q, contract_last,
                               preferred_element_type=F32)

    s = scores_t(km_ref[...], kam_ref[...])
    m = jnp.max(s, axis=0, keepdims=True)
    p = jnp.exp2((s - m) * EXP2_MULT)
    l = jnp.sum(p, axis=0, keepdims=True)
    acc = jnp.dot(vmt_ref[...], p.astype(BF16), preferred_element_type=F32)

    def block_scores(j):
        rows = pl.ds(pl.multiple_of(j * tk, tk), tk)
        return scores_t(k_ref[rows, :], ka_ref[rows, :])

    def absorb(j, s, m_prev, l_prev, acc_prev):
        m_new = jnp.maximum(m_prev, jnp.max(s, axis=0, keepdims=True))
        alpha = jnp.exp2((m_prev - m_new) * EXP2_MULT)
        p = jnp.exp2((s - m_new) * EXP2_MULT)
        l_new = alpha * l_prev + jnp.sum(p, axis=0, keepdims=True)
        acc_new = alpha * acc_prev + jnp.dot(vt_ref[j], p.astype(BF16),
                                             preferred_element_type=F32)
        return m_new, l_new, acc_new

    def step(j, carry):
        s_cur, m_prev, l_prev, acc_prev = carry
        s_next = block_scores(j + 1)
        return (s_next,) + absorb(j, s_cur, m_prev, l_prev, acc_prev)

    first = i * diag_blocks
    s_cur, m, l, acc = lax.fori_loop(0, first, step, (block_scores(0), m, l, acc))
    key = lax.broadcasted_iota(jnp.int32, s_cur.shape, 0)
    qry = lax.broadcasted_iota(jnp.int32, s_cur.shape, 1)
    for d in range(diag_blocks):
        s_next = block_scores(first + d + 1) if d + 1 < diag_blocks else None
        m, l, acc = absorb(first + d, jnp.where(key + d * tk <= qry, s_cur, NEG_BIG), m, l, acc)
        s_cur = s_next
    o_ref[...] = (acc / l).T.astype(o_ref.dtype)


def _attention(proj, proj_meta, vm_t, qa, ka, kam, tq, tk):
    t_real = proj.shape[0]
    kcol = ATT_WIDTH // HEAD_DIM
    vcol = 2 * ATT_WIDTH // HEAD_DIM
    return pl.pallas_call(
        functools.partial(_attn_body, tq=tq),
        grid=(N_HEADS, t_real // tq),
        in_specs=[
            pl.BlockSpec((tq, HEAD_DIM), lambda h, i: (i, h)),
            pl.BlockSpec((None, tq, LANES), lambda h, i: (h, i, 0)),
            pl.BlockSpec((t_real, HEAD_DIM), lambda h, i: (0, kcol + h)),
            pl.BlockSpec((None, t_real, LANES), lambda h, i: (h, 0, 0)),
            pl.BlockSpec((t_real, HEAD_DIM), lambda h, i: (0, vcol + h)),
            pl.BlockSpec((N_META, HEAD_DIM), lambda h, i: (0, kcol + h)),
            pl.BlockSpec((None, N_META, LANES), lambda h, i: (h, 0, 0)),
            pl.BlockSpec((None, HEAD_DIM, N_META), lambda h, i: (h, 0, 0)),
        ],
        out_specs=pl.BlockSpec((tq, HEAD_DIM), lambda h, i: (i, h)),
        out_shape=jax.ShapeDtypeStruct((t_real, ATT_WIDTH), BF16),
        scratch_shapes=[pltpu.VMEM((t_real // tk, HEAD_DIM, tk), BF16)],
        compiler_params=_params(("arbitrary", "arbitrary")),
        name="fox_attention",
    )(proj, qa, proj, ka, proj, proj_meta, kam, vm_t)


CONV_HIST = 32
CONV_ROWS = 128


def _glu(a_ref, g_ref, bglu_ref):
    a = a_ref[...].astype(F32) + bglu_ref[:, :CONV_CH]
    g = g_ref[...].astype(F32) + bglu_ref[:, CONV_CH:]
    return a * jax.nn.sigmoid(g)


def _conv_body(a_ref, g_ref, am_ref, gm_ref, bglu_ref, w_ref, cb_ref, gng_ref, gnb_ref,
               o_ref, ubuf, *, tm):
    i = pl.program_id(0)

    @pl.when(i == 0)
    def _():
        ubuf[0:CONV_HIST - N_META, :] = jnp.zeros((CONV_HIST - N_META, CONV_CH), F32)
        ubuf[CONV_HIST - N_META:CONV_HIST, :] = _glu(am_ref, gm_ref, bglu_ref)

    @pl.when(i > 0)
    def _():
        ubuf[0:CONV_HIST, :] = ubuf[tm:tm + CONV_HIST, :]

    ubuf[CONV_HIST:CONV_HIST + tm, :] = _glu(a_ref, g_ref, bglu_ref)

    first = CONV_HIST - (CONV_K - 1)
    for grp in range(N_GROUPS):
        lanes = slice(grp * GROUP_CH, (grp + 1) * GROUP_CH)
        for rc in range(tm // CONV_ROWS):
            base = rc * CONV_ROWS
            acc = jnp.zeros((CONV_ROWS, GROUP_CH), F32)
            for tap in range(CONV_K):
                acc = acc + ubuf[base + first + tap:base + first + tap + CONV_ROWS, lanes] * \
                    w_ref[tap:tap + 1, lanes]
            acc = acc + cb_ref[:, lanes]
            mu = jnp.mean(acc, axis=-1, keepdims=True)
            dlt = acc - mu
            var = jnp.mean(dlt * dlt, axis=-1, keepdims=True)
            y = dlt * lax.rsqrt(var + NORM_EPS) * gng_ref[:, lanes] + gnb_ref[:, lanes]
            o_ref[base:base + CONV_ROWS, lanes] = (y * jax.nn.sigmoid(y)).astype(o_ref.dtype)


def _conv_mixer(proj, proj_meta, b_glu, conv_w, conv_b, gn_g, gn_b, tm):
    t_real = proj.shape[0]
    acol = 3 * ATT_WIDTH // CONV_CH
    gcol = acol + 1
    const = lambda i: (0, 0)
    return pl.pallas_call(
        functools.partial(_conv_body, tm=tm),
        grid=(t_real // tm,),
        in_specs=[
            pl.BlockSpec((tm, CONV_CH), lambda i: (i, acol)),
            pl.BlockSpec((tm, CONV_CH), lambda i: (i, gcol)),
            pl.BlockSpec((N_META, CONV_CH), lambda i: (0, acol)),
            pl.BlockSpec((N_META, CONV_CH), lambda i: (0, gcol)),
            pl.BlockSpec((1, 2 * CONV_CH), const),
            pl.BlockSpec((CONV_K, CONV_CH), const),
            pl.BlockSpec((1, CONV_CH), const),
            pl.BlockSpec((1, CONV_CH), const),
            pl.BlockSpec((1, CONV_CH), const),
        ],
        out_specs=pl.BlockSpec((tm, CONV_CH), lambda i: (i, 0)),
        out_shape=jax.ShapeDtypeStruct((t_real, CONV_CH), BF16),
        scratch_shapes=[pltpu.VMEM((tm + CONV_HIST, CONV_CH), F32)],
        compiler_params=_params(("arbitrary",)),
        name="conv_mixer",
    )(proj, proj, proj_meta, proj_meta, b_glu, conv_w, conv_b, gn_g, gn_b)


def _pack_bf16_pairs(hi_half, lo_half):
    lo = lax.bitcast_convert_type(lo_half.astype(F32), U32) >> 16
    hi = lax.bitcast_convert_type(hi_half.astype(F32), U32) & jnp.uint32(HI_MASK)
    return lo | hi


def _unpack_bf16_pairs(words):
    lo = lax.bitcast_convert_type(words << 16, F32).astype(BF16)
    hi = lax.bitcast_convert_type(words & jnp.uint32(HI_MASK), F32).astype(BF16)
    return jnp.concatenate([lo, hi], axis=1)


def _outproj_body(att_ref, conv_ref, x_ref, w_ref, g_ref, rwh_ref, rwl_ref, rb_ref,
                  h_ref, hnp_ref, topi_ref, gate_ref):
    mix = jnp.dot(att_ref[...], w_ref[0:ATT_WIDTH, :], preferred_element_type=F32)
    mix = mix + jnp.dot(conv_ref[...], w_ref[ATT_WIDTH:, :], preferred_element_type=F32)
    h = x_ref[...] + mix
    h_ref[...] = h
    inv = lax.rsqrt(jnp.mean(h * h, axis=-1, keepdims=True) + NORM_EPS)
    hn = h * inv * g_ref[...]
    hn_hi = hn.astype(BF16)
    hnp_ref[...] = _pack_bf16_pairs(hn_hi[:, HALF:], hn_hi[:, :HALF])

    hn_lo = (hn - hn_hi.astype(F32)).astype(BF16)
    logits = jnp.dot(hn_hi, rwh_ref[...], preferred_element_type=F32)
    logits = logits + jnp.dot(hn_hi, rwl_ref[...], preferred_element_type=F32)
    logits = logits + jnp.dot(hn_lo, rwh_ref[...], preferred_element_type=F32) + rb_ref[...]
    lane = lax.broadcasted_iota(jnp.int32, logits.shape, 1)
    vals = logits
    tops = []
    idxs = []
    for _ in range(TOP_K):
        mx = jnp.max(vals, axis=-1, keepdims=True)
        ix = jnp.min(jnp.where(vals == mx, lane, LANES), axis=-1, keepdims=True)
        tops.append(mx)
        idxs.append(ix)
        vals = jnp.where(lane == ix, -jnp.inf, vals)
    exps = [jnp.exp(v - tops[0]) for v in tops]
    denom = exps[0] + exps[1] + exps[2] + exps[3]
    topi = jnp.full(logits.shape, -1, jnp.int32)
    gate = jnp.zeros(logits.shape, F32)
    for k in range(TOP_K):
        topi = jnp.where(lane == k, idxs[k], topi)
        gate = jnp.where(lane == k, exps[k] / denom, gate)
    topi_ref[...] = topi
    gate_ref[...] = gate


def _outproj_router(att, conv, x, w_out, g, rw_hi, rw_lo, router_b, tm):
    t_real, d = x.shape
    const = lambda i: (0, 0)
    return pl.pallas_call(
        _outproj_body,
        grid=(t_real // tm,),
        in_specs=[
            pl.BlockSpec((tm, ATT_WIDTH), lambda i: (i, 0)),
            pl.BlockSpec((tm, CONV_CH), lambda i: (i, 0)),
            pl.BlockSpec((tm, d), lambda i: (i, 0)),
            pl.BlockSpec((d, d), const),
            pl.BlockSpec((1, d), const),
            pl.BlockSpec((d, LANES), const),
            pl.BlockSpec((d, LANES), const),
            pl.BlockSpec((1, LANES), const),
        ],
        out_specs=[
            pl.BlockSpec((tm, d), lambda i: (i, 0)),
            pl.BlockSpec((tm, HALF), lambda i: (i, 0)),
            pl.BlockSpec((tm, LANES), lambda i: (i, 0)),
            pl.BlockSpec((tm, LANES), lambda i: (i, 0)),
        ],
        out_shape=[
            jax.ShapeDtypeStruct((t_real, d), F32),
            jax.ShapeDtypeStruct((t_real, HALF), U32),
            jax.ShapeDtypeStruct((t_real, LANES), jnp.int32),
            jax.ShapeDtypeStruct((t_real, LANES), F32),
        ],
        compiler_params=_params(("arbitrary",)),
        name="outproj_router",
    )(att, conv, x, w_out, g, rw_hi, rw_lo, router_b)


MOE_TM = 256
PLAN_TB = 256
PLAN_ROWS = SUBLANES


def _plan_body(topi_ref, dest_ref, meta_ref, cnt_ref, carry_ref, start_ref):
    phase = pl.program_id(0)
    j = pl.program_id(1)
    topi = topi_ref[...]
    lane = lax.broadcasted_iota(jnp.int32, topi.shape, 1)
    picks = [topi[:, k:k + 1] for k in range(TOP_K)]
    onehot = jnp.zeros(topi.shape, F32)
    for pick in picks:
        onehot = onehot + (lane == pick).astype(F32)
    colsum = jnp.sum(onehot, axis=0, keepdims=True)

    @pl.when((phase == 0) & (j == 0))
    def _():
        cnt_ref[...] = jnp.zeros(cnt_ref.shape, F32)

    @pl.when(phase == 0)
    def _():
        cnt_ref[...] += colsum

    @pl.when((phase == 1) & (j == 0))
    def _():
        cnt = jnp.broadcast_to(cnt_ref[...], (SUBLANES, LANES))
        tiles = jnp.floor((cnt + (MOE_TM - 1)) * (1.0 / MOE_TM))
        padded = tiles * MOE_TM
        row = lax.broadcasted_iota(jnp.int32, (LANES, LANES), 0)
        col = lax.broadcasted_iota(jnp.int32, (LANES, LANES), 1)
        upper = (row <= col).astype(F32)
        pad_end = jnp.dot(padded, upper, preferred_element_type=F32,
                          precision=lax.Precision.HIGHEST)
        start = pad_end - padded
        start_ref[...] = start[0:1, :]
        carry_ref[...] = jnp.zeros(carry_ref.shape, F32)
        sub = lax.broadcasted_iota(jnp.int32, (PLAN_ROWS, LANES), 0)
        meta_ref[...] = jnp.where(sub == 0, cnt, jnp.where(sub == 1, start, tiles)).astype(jnp.int32)

    @pl.when(phase == 1)
    def _():
        row = lax.broadcasted_iota(jnp.int32, (PLAN_TB, PLAN_TB), 0)
        col = lax.broadcasted_iota(jnp.int32, (PLAN_TB, PLAN_TB), 1)
        earlier = (col < row).astype(BF16)
        before = jnp.dot(earlier, onehot.astype(BF16), preferred_element_type=F32)
        slot = before + carry_ref[...] + start_ref[...]
        dest = jnp.zeros(topi.shape, jnp.int32)
        for k, pick in enumerate(picks):
            d_k = jnp.sum(jnp.where(lane == pick, slot, 0.0), axis=-1, keepdims=True)
            dest = jnp.where(lane == k, d_k.astype(jnp.int32), dest)
        dest_ref[...] = dest
        carry_ref[...] += colsum


def _routing_plan(top_i):
    t_real = top_i.shape[0]
    return pl.pallas_call(
        _plan_body,
        grid=(2, t_real // PLAN_TB),
        in_specs=[pl.BlockSpec((PLAN_TB, LANES), lambda p, j: (j, 0))],
        out_specs=[
            pl.BlockSpec((PLAN_TB, LANES), lambda p, j: (j * p, 0)),
            pl.BlockSpec((PLAN_ROWS, LANES), lambda p, j: (0, 0)),
        ],
        out_shape=[
            jax.ShapeDtypeStruct((t_real, LANES), jnp.int32),
            jax.ShapeDtypeStruct((PLAN_ROWS, LANES), jnp.int32),
        ],
        scratch_shapes=[pltpu.VMEM((1, LANES), F32)] * 3,
        compiler_params=_params(("arbitrary", "arbitrary")),
        name="routing_plan",
    )(top_i)


MOE_ITEM_TILES = 5
MOE_ITEM_ROWS = MOE_ITEM_TILES * MOE_TM


def _item_tables(counts, starts, tiles_e, n_items):
    items_e = (tiles_e + MOE_ITEM_TILES - 1) // MOE_ITEM_TILES
    item_end = jnp.cumsum(items_e)
    item_start = item_end - items_e
    wid = jnp.arange(n_items, dtype=jnp.int32)
    e_of = jnp.minimum(jnp.sum((item_end[None, :] <= wid[:, None]).astype(jnp.int32), axis=1),
                       N_EXPERTS - 1)
    k_of = wid - item_start[e_of]
    live = wid < item_end[-1]
    last_e = jnp.max(jnp.where(counts > 0, jnp.arange(N_EXPERTS, dtype=jnp.int32), 0))
    item_e = jnp.where(live, e_of, last_e).astype(jnp.int32)
    item_nt = jnp.where(live, jnp.minimum(MOE_ITEM_TILES, tiles_e[e_of] - k_of * MOE_ITEM_TILES), 0)
    item_row0 = jnp.where(live, starts[e_of] + k_of * MOE_ITEM_ROWS, 0)
    return item_e, item_row0.astype(jnp.int32), item_nt.astype(jnp.int32)


DISP_TB = 256


def _dispatch_body(cnt_ref, start_ref, tiles_ref, nused_ref, dest_ref, src_ref, xs_hbm,
                   rows, zrow, ztile, sem, zsem):
    t = pl.program_id(0)
    n_steps = pl.num_programs(0)
    n_tiles = xs_hbm.shape[0] // MOE_TM
    pushes = DISP_TB * TOP_K
    slot = t % 2
    rows[slot] = src_ref[...]

    def issue(r, carry):
        for k in range(TOP_K):
            pltpu.make_async_copy(
                rows.at[slot, pl.ds(r, 1), :],
                xs_hbm.at[pl.ds(dest_ref[0, r * TOP_K + k], 1), :],
                sem.at[slot]).start(priority=k % 2)
        return carry

    lax.fori_loop(0, DISP_TB, issue, 0)

    def retire_step(buf):
        pltpu.make_async_copy(xs_hbm.at[pl.ds(0, pushes), :], xs_hbm.at[pl.ds(0, pushes), :],
                              sem.at[buf]).wait()

    @pl.when(t > 0)
    def _():
        retire_step(1 - slot)

    @pl.when(t == n_steps - 1)
    def _():
        retire_step(slot)
        zrow[...] = jnp.zeros(zrow.shape, zrow.dtype)
        ztile[...] = jnp.zeros(ztile.shape, ztile.dtype)

        def zero_row(r):
            return pltpu.make_async_copy(zrow, xs_hbm.at[pl.ds(r, 1), :], zsem)

        def per_expert(e, carry):
            lo = start_ref[e] + cnt_ref[e]
            hi = start_ref[e] + tiles_ref[e] * MOE_TM
            lax.fori_loop(lo, hi, lambda r, c: (zero_row(r).start(), c)[1], 0)
            lax.fori_loop(lo, hi, lambda r, c: (zero_row(r).wait(), c)[1], 0)
            return carry

        lax.fori_loop(0, N_EXPERTS, per_expert, 0)

        def zero_tile(tix, carry):
            cp = pltpu.make_async_copy(
                ztile, xs_hbm.at[pl.ds(pl.multiple_of(tix * MOE_TM, MOE_TM), MOE_TM), :], zsem)
            cp.start()
            cp.wait()
            return carry

        lax.fori_loop(nused_ref[0], n_tiles, zero_tile, 0)


def _dispatch(counts, starts, tiles_e, n_used, dest4, hn_packed, n_tiles):
    t_real = hn_packed.shape[0]
    n_steps = t_real // DISP_TB
    grid_spec = pltpu.PrefetchScalarGridSpec(
        num_scalar_prefetch=4,
        grid=(n_steps,),
        in_specs=[
            pl.BlockSpec((None, 1, DISP_TB * TOP_K), lambda t, *_: (t, 0, 0),
                         memory_space=pltpu.SMEM),
            pl.BlockSpec((DISP_TB, HALF), lambda t, *_: (t, 0)),
        ],
        out_specs=pl.BlockSpec(memory_space=pl.ANY),
        scratch_shapes=[
            pltpu.VMEM((2, DISP_TB, HALF), U32),
            pltpu.VMEM((1, HALF), U32),
            pltpu.VMEM((MOE_TM, HALF), U32),
            pltpu.SemaphoreType.DMA((2,)),
            pltpu.SemaphoreType.DMA(()),
        ],
    )
    return pl.pallas_call(
        _dispatch_body,
        grid_spec=grid_spec,
        out_shape=jax.ShapeDtypeStruct((n_tiles * MOE_TM, HALF), U32),
        compiler_params=_params(("arbitrary",)),
        name="moe_dispatch",
    )(counts, starts, tiles_e, n_used, dest4.reshape(n_steps, 1, DISP_TB * TOP_K), hn_packed)


MOE_TF = 512
MOE_NF = D_FF // MOE_TF


def _moe_body(e_ref, row0_ref, nt_ref, nused_ref, xs_hbm, wg_ref, wu_ref, bg_ref, bu_ref,
              wd_ref, bd_ref, y_hbm, xq, yacc, sem_x, sem_y):
    w = pl.program_id(0)
    f = pl.program_id(1)
    n_items = pl.num_programs(0)
    nt = nt_ref[w]
    slot = w % 2

    def x_copy(item, r, buf):
        return pltpu.make_async_copy(
            xs_hbm.at[pl.ds(pl.multiple_of(row0_ref[item] + r * MOE_TM, MOE_TM), MOE_TM), :],
            xq.at[buf, pl.ds(r * MOE_TM, MOE_TM), :], sem_x.at[buf])

    def y_copy(r):
        return pltpu.make_async_copy(
            yacc.at[pl.ds(pl.multiple_of(r * MOE_TM, MOE_TM), MOE_TM), :],
            y_hbm.at[pl.ds(pl.multiple_of(row0_ref[w] + r * MOE_TM, MOE_TM), MOE_TM), :], sem_y)

    def for_tiles(item, fn):
        for r in range(MOE_ITEM_TILES):
            @pl.when(r < nt_ref[item])
            def _():
                fn(r)

    @pl.when(f == 0)
    def _():
        @pl.when(w == 0)
        def _():
            for_tiles(0, lambda r: x_copy(0, r, 0).start())

        for_tiles(w, lambda r: x_copy(w, r, slot).wait())

        @pl.when(w + 1 < n_items)
        def _():
            nxt = jnp.minimum(w + 1, n_items - 1)
            for_tiles(nxt, lambda r: x_copy(nxt, r, 1 - slot).start())

    @pl.when(nt > 0)
    def _():
        def tile_rows(r):
            return pl.ds(pl.multiple_of(r * MOE_TM, MOE_TM), MOE_TM)

        def hidden(r):
            x = _unpack_bf16_pairs(xq[slot, tile_rows(r), :])
            gate = jnp.dot(x, wg_ref[...].astype(BF16), preferred_element_type=F32) + bg_ref[...]
            up = jnp.dot(x, wu_ref[...].astype(BF16), preferred_element_type=F32) + bu_ref[...]
            gate = jnp.minimum(gate, SWIGLU_LIMIT)
            up = jnp.clip(up, -SWIGLU_LIMIT, SWIGLU_LIMIT)
            return (gate * jax.nn.sigmoid(SWIGLU_ALPHA * gate) * (up + 1.0)).astype(BF16)

        def project(r, act):
            yacc[tile_rows(r), :] += jnp.dot(act, wd_ref[...].astype(BF16),
                                             preferred_element_type=F32)

            @pl.when(f == MOE_NF - 1)
            def _():
                y_copy(r).start()

        def tile(r, act_prev):
            act = hidden(r)
            project(r - 1, act_prev)
            return act

        @pl.when(f == 0)
        def _():
            def seed(r, carry):
                yacc[tile_rows(r), :] = jnp.broadcast_to(bd_ref[...], (MOE_TM, D_MODEL))
                return carry

            lax.fori_loop(0, nt, seed, 0)

        project(nt - 1, lax.fori_loop(1, nt, tile, hidden(0)))

        @pl.when(f == MOE_NF - 1)
        def _():
            for_tiles(w, lambda r: y_copy(r).wait())

    @pl.when((w == n_items - 1) & (f == MOE_NF - 1))
    def _():
        yacc[0:MOE_TM, :] = jnp.zeros((MOE_TM, D_MODEL), F32)

        def zero_tile(tix, carry):
            cp = pltpu.make_async_copy(
                yacc.at[pl.ds(0, MOE_TM), :],
                y_hbm.at[pl.ds(pl.multiple_of(tix * MOE_TM, MOE_TM), MOE_TM), :], sem_y)
            cp.start()
            cp.wait()
            return carry

        lax.fori_loop(nused_ref[0], y_hbm.shape[0] // MOE_TM, zero_tile, 0)


def _moe_ffn(item_e, item_row0, item_nt, n_used, xs, w_gate_up, b_gate_up, w_down, b_down, n_items):
    n_rows = xs.shape[0]

    def fidx(w, f, nt):
        return jnp.where(nt[w] > 0, f, MOE_NF - 1)

    grid_spec = pltpu.PrefetchScalarGridSpec(
        num_scalar_prefetch=4,
        grid=(n_items, MOE_NF),
        in_specs=[
            pl.BlockSpec(memory_space=pl.ANY),
            pl.BlockSpec((None, D_MODEL, MOE_TF),
                         lambda w, f, e, r0, nt, nu: (e[w], 0, fidx(w, f, nt))),
            pl.BlockSpec((None, D_MODEL, MOE_TF),
                         lambda w, f, e, r0, nt, nu: (e[w], 0, MOE_NF + fidx(w, f, nt))),
            pl.BlockSpec((None, 1, MOE_TF), lambda w, f, e, r0, nt, nu: (e[w], 0, fidx(w, f, nt))),
            pl.BlockSpec((None, 1, MOE_TF),
                         lambda w, f, e, r0, nt, nu: (e[w], 0, MOE_NF + fidx(w, f, nt))),
            pl.BlockSpec((None, MOE_TF, D_MODEL),
                         lambda w, f, e, r0, nt, nu: (e[w], fidx(w, f, nt), 0)),
            pl.BlockSpec((None, 1, D_MODEL), lambda w, f, e, r0, nt, nu: (e[w], 0, 0)),
        ],
        out_specs=pl.BlockSpec(memory_space=pl.ANY),
        scratch_shapes=[
            pltpu.VMEM((2, MOE_ITEM_ROWS, HALF), U32),
            pltpu.VMEM((MOE_ITEM_ROWS, D_MODEL), F32),
            pltpu.SemaphoreType.DMA((2,)),
            pltpu.SemaphoreType.DMA(()),
        ],
    )
    return pl.pallas_call(
        _moe_body,
        grid_spec=grid_spec,
        out_shape=jax.ShapeDtypeStruct((n_rows, D_MODEL), F32),
        compiler_params=_params(("arbitrary", "arbitrary")),
        name="moe_ffn",
    )(item_e, item_row0, item_nt, n_used, xs, w_gate_up, w_gate_up,
      b_gate_up.reshape(N_EXPERTS, 1, 2 * D_FF), b_gate_up.reshape(N_EXPERTS, 1, 2 * D_FF),
      w_down, b_down.reshape(N_EXPERTS, 1, D_MODEL))


COMB_TM = 128


def _combine_body(dest_ref, dnext_ref, y_hbm, gate_ref, h_ref, g_ref, o_ref, stage, sem):
    t = pl.program_id(0)
    n_steps = pl.num_programs(0)
    slot = t % 2

    def pull(idx_ref, r, k, buf):
        return pltpu.make_async_copy(
            y_hbm.at[pl.ds(idx_ref[0, r * TOP_K + k], 1), :],
            stage.at[buf, pl.ds(k * COMB_TM + r, 1), :], sem.at[buf])

    def issue_all(idx_ref, buf):
        def issue(r, carry):
            for k in range(TOP_K):
                pull(idx_ref, r, k, buf).start(priority=k % 2)
            return carry

        lax.fori_loop(0, COMB_TM, issue, 0)

    @pl.when(t == 0)
    def _():
        issue_all(dest_ref, 0)

    @pl.when(t + 1 < n_steps)
    def _():
        issue_all(dnext_ref, 1 - slot)

    pltpu.make_async_copy(y_hbm.at[pl.ds(0, TOP_K * COMB_TM), :], stage.at[slot],
                          sem.at[slot]).wait()

    gates = gate_ref[...]
    ff = gates[:, 0:1] * stage[slot, 0:COMB_TM, :]
    for k in range(1, TOP_K):
        ff = ff + gates[:, k:k + 1] * stage[slot, k * COMB_TM:(k + 1) * COMB_TM, :]
    h = h_ref[...] + ff
    inv = lax.rsqrt(jnp.mean(h * h, axis=-1, keepdims=True) + NORM_EPS)
    o_ref[...] = h * inv * g_ref[...]


def _combine(dest4, y_rows, gates, h1, g):
    t_real, d = h1.shape
    n_steps = t_real // COMB_TM
    dest_blocks = dest4.reshape(n_steps, 1, COMB_TM * TOP_K)
    smem_block = (None, 1, COMB_TM * TOP_K)
    return pl.pallas_call(
        _combine_body,
        grid=(n_steps,),
        in_specs=[
            pl.BlockSpec(smem_block, lambda i: (i, 0, 0), memory_space=pltpu.SMEM),
            pl.BlockSpec(smem_block, lambda i: (jnp.minimum(i + 1, n_steps - 1), 0, 0),
                         memory_space=pltpu.SMEM),
            pl.BlockSpec(memory_space=pl.ANY),
            pl.BlockSpec((COMB_TM, LANES), lambda i: (i, 0)),
            pl.BlockSpec((COMB_TM, d), lambda i: (i, 0)),
            pl.BlockSpec((1, d), lambda i: (0, 0)),
        ],
        out_specs=pl.BlockSpec((COMB_TM, d), lambda i: (i, 0)),
        out_shape=jax.ShapeDtypeStruct((t_real, d), F32),
        scratch_shapes=[
            pltpu.VMEM((2, TOP_K * COMB_TM, d), F32),
            pltpu.SemaphoreType.DMA((2,)),
        ],
        compiler_params=_params(("arbitrary",)),
        name="moe_combine",
    )(dest_blocks, dest_blocks, y_rows, gates, h1, g)


def kernel(x, meta_tokens, norm_mix_g, w_in, b_fgate, b_glu, conv_w, conv_b, gn_g, gn_b, w_out,
           norm_ffn_g, router_w, router_b, w_gate_up, b_gate_up, w_down, b_down, norm_final_g):
    assert x.shape[0] == 1 and norm_mix_g.shape[0] == 1
    xt = x[0]
    t_real = xt.shape[0]
    fcol = 3 * ATT_WIDTH
    w_in0 = w_in[0]
    w_main = jnp.concatenate([w_in0[:, :fcol], w_in0[:, fcol + N_HEADS:]], axis=1).astype(BF16)
    w_f = jnp.pad(w_in0[:, fcol:fcol + N_HEADS], ((0, 0), (0, LANES - N_HEADS))).astype(BF16)
    g_mix = norm_mix_g[0].reshape(1, D_MODEL)

    proj, f_real = _inproj(xt, g_mix, w_main, w_f, tm=1024, tn=512)
    proj_meta, f_meta = _inproj(meta_tokens, g_mix, w_main, w_f, tm=N_META, tn=512)

    b_f = jnp.pad(b_fgate[0], (0, LANES - N_HEADS)).reshape(1, LANES)
    qa, ka, kam = _fgate_bias(f_meta, f_real, b_f)
    v_meta = proj_meta[:, 2 * ATT_WIDTH:3 * ATT_WIDTH]
    vm_t = v_meta.reshape(N_META, N_HEADS, HEAD_DIM).transpose(1, 2, 0)
    att = _attention(proj, proj_meta, vm_t, qa, ka, kam, tq=512, tk=512)
    conv = _conv_mixer(proj, proj_meta, b_glu[0].reshape(1, -1), conv_w[0],
                       conv_b[0].reshape(1, -1), gn_g[0].reshape(1, -1), gn_b[0].reshape(1, -1),
                       tm=512)

    rw = jnp.pad(router_w[0], ((0, 0), (0, LANES - N_EXPERTS)))
    rw_hi = rw.astype(BF16)
    rw_lo = (rw - rw_hi.astype(F32)).astype(BF16)
    rb = jnp.pad(router_b[0], (0, LANES - N_EXPERTS), constant_values=NEG_BIG).reshape(1, LANES)
    h1, hn_packed, top_i, gates = _outproj_router(
        att, conv, xt, w_out[0].astype(BF16), norm_ffn_g[0].reshape(1, D_MODEL),
        rw_hi, rw_lo, rb, tm=256)

    n_assign = t_real * TOP_K
    n_tiles = (n_assign + N_EXPERTS * (MOE_TM - 1)) // MOE_TM + 1
    n_items = N_EXPERTS + (n_tiles * MOE_TM) // MOE_ITEM_ROWS
    dest, plan = _routing_plan(top_i)
    counts, starts, tiles_e = (plan[r, :N_EXPERTS] for r in range(3))
    n_used = jnp.sum(tiles_e).astype(jnp.int32).reshape(1)
    item_e, item_row0, item_nt = _item_tables(counts, starts, tiles_e, n_items)
    dest4 = dest[:, :TOP_K]

    xs = _dispatch(counts, starts, tiles_e, n_used, dest4, hn_packed, n_tiles)
    y_rows = _moe_ffn(item_e, item_row0, item_nt, n_used, xs, w_gate_up[0], b_gate_up[0],
                      w_down[0], b_down[0], n_items)
    out = _combine(dest4, y_rows, gates, h1, norm_final_g.reshape(1, D_MODEL))
    return out[None]
```

```python
import functools

import jax
import jax.numpy as jnp
import numpy as np
from jax import lax
from jax.experimental import pallas as pl
from jax.experimental.pallas import tpu as pltpu

D_MODEL = 2048
N_META = 16
HEAD_DIM = 128
ATT_WIDTH = 1024
N_HEADS = ATT_WIDTH // HEAD_DIM
CONV_CH = 1024
N_GROUPS = 8
GROUP_CH = CONV_CH // N_GROUPS
CONV_K = 31
N_EXPERTS = 32
TOP_K = 4
D_FF = 2048
SWIGLU_LIMIT = 7.0
SWIGLU_ALPHA = 1.702
NORM_EPS = 1e-5
NEG_BIG = -1e30

LANES = 128
SUBLANES = 8
VMEM_LIMIT = 56 * 1024 * 1024
HALF = D_MODEL // 2

F32 = jnp.float32
BF16 = jnp.bfloat16
U32 = jnp.uint32
HI_MASK = 0xFFFF0000


def _params(sem, vmem=VMEM_LIMIT):
    return pltpu.CompilerParams(dimension_semantics=sem, vmem_limit_bytes=vmem)


def _inproj_body(x_ref, g_ref, w_ref, wf_ref, o_ref, f_ref, hn_ref):
    @pl.when(pl.program_id(1) == 0)
    def _():
        x = x_ref[...]
        inv = lax.rsqrt(jnp.mean(x * x, axis=-1, keepdims=True) + NORM_EPS)
        hn = (x * inv * g_ref[...]).astype(BF16)
        hn_ref[...] = hn
        f_ref[...] = jnp.dot(hn, wf_ref[...], preferred_element_type=F32)

    o_ref[...] = jnp.dot(hn_ref[...], w_ref[...], preferred_element_type=F32).astype(o_ref.dtype)


def _inproj(x, g, w_main, w_f, tm, tn):
    m, d = x.shape
    n = w_main.shape[1]
    return pl.pallas_call(
        _inproj_body,
        grid=(m // tm, n // tn),
        in_specs=[
            pl.BlockSpec((tm, d), lambda i, j: (i, 0)),
            pl.BlockSpec((1, d), lambda i, j: (0, 0)),
            pl.BlockSpec((d, tn), lambda i, j: (0, j)),
            pl.BlockSpec((d, LANES), lambda i, j: (0, 0)),
        ],
        out_specs=[
            pl.BlockSpec((tm, tn), lambda i, j: (i, j)),
            pl.BlockSpec((tm, LANES), lambda i, j: (i, 0)),
        ],
        out_shape=[
            jax.ShapeDtypeStruct((m, n), BF16),
            jax.ShapeDtypeStruct((m, LANES), F32),
        ],
        scratch_shapes=[pltpu.VMEM((tm, d), BF16)],
        compiler_params=_params(("arbitrary", "arbitrary")),
        name="inproj",
    )(x, g, w_main, w_f)


ATT_SCALE = 1.0 / float(np.sqrt(HEAD_DIM))
EXP2_MULT = ATT_SCALE * float(np.log2(np.e))
N_SPLIT = 3
FGATE_TB = 512


def _log_sigmoid(x):
    return jnp.minimum(x, 0.0) - jnp.log1p(jnp.exp(-jnp.abs(x)))


def _prefix_rows(lf):
    n = lf.shape[0]
    row = lax.broadcasted_iota(jnp.int32, (n, n), 0)
    col = lax.broadcasted_iota(jnp.int32, (n, n), 1)
    tri = (col <= row).astype(F32)
    return jnp.dot(tri, lf, preferred_element_type=F32, precision=lax.Precision.HIGHEST)


def _bias_columns(c, head):
    lane = lax.broadcasted_iota(jnp.int32, c.shape, 1)
    x = jnp.sum(jnp.where(lane == head, c, 0.0), axis=-1, keepdims=True) * float(np.sqrt(HEAD_DIM))
    hi = x.astype(BF16).astype(F32)
    r1 = x - hi
    mid = r1.astype(BF16).astype(F32)
    lo = r1 - mid
    part = lane % N_SPLIT
    parts = jnp.where(part == 0, hi, jnp.where(part == 1, mid, lo))
    qa = jnp.where(lane < N_SPLIT, parts, jnp.where(lane < 2 * N_SPLIT, 1.0, 0.0))
    ka = jnp.where(lane < N_SPLIT, 1.0, jnp.where(lane < 2 * N_SPLIT, -parts, 0.0))
    return qa.astype(BF16), ka.astype(BF16)


def _fgate_body(fm_ref, f_ref, b_ref, qa_ref, ka_ref, kam_ref, carry_ref):
    @pl.when(pl.program_id(0) == 0)
    def _():
        cm = _prefix_rows(_log_sigmoid(fm_ref[...] + b_ref[...]))
        carry_ref[...] = cm[N_META - 1:N_META, :]
        for head in range(N_HEADS):
            kam_ref[head] = _bias_columns(cm, head)[1]

    c = _prefix_rows(_log_sigmoid(f_ref[...] + b_ref[...])) + carry_ref[...]
    carry_ref[...] = c[FGATE_TB - 1:FGATE_TB, :]
    for head in range(N_HEADS):
        qa, ka = _bias_columns(c, head)
        qa_ref[head] = qa
        ka_ref[head] = ka


def _fgate_bias(f_meta, f_real, b_f):
    t_real = f_real.shape[0]
    return pl.pallas_call(
        _fgate_body,
        grid=(t_real // FGATE_TB,),
        in_specs=[
            pl.BlockSpec((N_META, LANES), lambda i: (0, 0)),
            pl.BlockSpec((FGATE_TB, LANES), lambda i: (i, 0)),
            pl.BlockSpec((1, LANES), lambda i: (0, 0)),
        ],
        out_specs=[
            pl.BlockSpec((N_HEADS, FGATE_TB, LANES), lambda i: (0, i, 0)),
            pl.BlockSpec((N_HEADS, FGATE_TB, LANES), lambda i: (0, i, 0)),
            pl.BlockSpec((N_HEADS, N_META, LANES), lambda i: (0, 0, 0)),
        ],
        out_shape=[
            jax.ShapeDtypeStruct((N_HEADS, t_real, LANES), BF16),
            jax.ShapeDtypeStruct((N_HEADS, t_real, LANES), BF16),
            jax.ShapeDtypeStruct((N_HEADS, N_META, LANES), BF16),
        ],
        scratch_shapes=[pltpu.VMEM((1, LANES), F32)],
        compiler_params=_params(("arbitrary",)),
        name="fgate_bias",
    )(f_meta, f_real, b_f)


ATT_PACK_ROWS = 16
ATT_HEADS_PER_STEP = 2


def _attn_body(q_ref, qa_ref, k_ref, ka_ref, v_ref, km_ref, kam_ref, vmt_ref, o_ref,
               vt_ref, s_a, s_b, p_ref, acc_ref, *, tq):
    i = pl.program_id(1)
    heads, n_blocks, _, tk = vt_ref.shape
    assert tq == 2 * tk
    contract_last = (((1,), (1,)), ((), ()))

    def cols(hh):
        return slice(hh * HEAD_DIM, (hh + 1) * HEAD_DIM)

    @pl.when(i == 0)
    def _():
        for hh in range(heads):
            for blk in range(n_blocks):
                vt_ref[hh, blk] = v_ref[blk * tk:(blk + 1) * tk, cols(hh)].T

    q = [jnp.concatenate([q_ref[:, cols(hh)], qa_ref[hh]], axis=1) for hh in range(heads)]

    def scores_t(hh, k, ka):
        return lax.dot_general(jnp.concatenate([k, ka], axis=1), q[hh], contract_last,
                               preferred_element_type=F32)

    def block_scores(hh, j):
        rows = pl.ds(pl.multiple_of(j * tk, tk), tk)
        return scores_t(hh, k_ref[rows, cols(hh)], ka_ref[hh, rows, :])

    stats = []
    for hh in range(heads):
        s = scores_t(hh, km_ref[:, cols(hh)], kam_ref[hh])
        m = jnp.max(s, axis=0, keepdims=True)
        p = jnp.exp2((s - m) * EXP2_MULT)
        stats += [m, jnp.sum(p, axis=0, keepdims=True)]
        acc_ref[hh] = jnp.dot(vmt_ref[hh], p.astype(BF16), preferred_element_type=F32)

    def absorb(hh, s_ref, j, m_prev, l_prev):
        top = s_ref[hh, 0:SUBLANES, :]
        for r in range(SUBLANES, tk, SUBLANES):
            top = jnp.maximum(top, s_ref[hh, r:r + SUBLANES, :])
        m_new = jnp.maximum(m_prev, jnp.max(top, axis=0, keepdims=True))
        alpha = jnp.exp2((m_prev - m_new) * EXP2_MULT)
        part = jnp.zeros((SUBLANES, tq), F32)
        for r in range(0, tk, ATT_PACK_ROWS):
            p = jnp.exp2((s_ref[hh, r:r + ATT_PACK_ROWS, :] - m_new) * EXP2_MULT)
            part = part + p[0:SUBLANES, :] + p[SUBLANES:, :]
            p_ref[hh, r:r + ATT_PACK_ROWS, :] = p.astype(BF16)
        l_new = alpha * l_prev + jnp.sum(part, axis=0, keepdims=True)
        acc_ref[hh] = alpha * acc_ref[hh] + jnp.dot(vt_ref[hh, j], p_ref[hh],
                                                    preferred_element_type=F32)
        return [m_new, l_new]

    def absorb_all(s_ref, j, st):
        out = []
        for hh in range(heads):
            out += absorb(hh, s_ref, j, st[2 * hh], st[2 * hh + 1])
        return out

    def pair(pi, st):
        for hh in range(heads):
            s_b[hh] = block_scores(hh, 2 * pi + 1)
        st = absorb_all(s_a, 2 * pi, st)
        for hh in range(heads):
            s_a[hh] = block_scores(hh, 2 * pi + 2)
        return tuple(absorb_all(s_b, 2 * pi + 1, st))

    for hh in range(heads):
        s_a[hh] = block_scores(hh, 0)
    stats = lax.fori_loop(0, i, pair, tuple(stats))

    key = lax.broadcasted_iota(jnp.int32, (tk, tq), 0)
    qry = lax.broadcasted_iota(jnp.int32, (tk, tq), 1)
    for hh in range(heads):
        s_b[hh] = jnp.where(key + tk <= qry, block_scores(hh, 2 * i + 1), NEG_BIG)
        s_a[hh] = jnp.where(key <= qry, s_a[hh], NEG_BIG)
    stats = absorb_all(s_b, 2 * i + 1, absorb_all(s_a, 2 * i, stats))
    for hh in range(heads):
        o_ref[:, cols(hh)] = (acc_ref[hh] / stats[2 * hh + 1]).T.astype(o_ref.dtype)


def _attention(proj, proj_meta, vm_t, qa, ka, kam, tq, tk):
    t_real = proj.shape[0]
    hs = ATT_HEADS_PER_STEP
    width = hs * HEAD_DIM
    kcol = ATT_WIDTH // width
    vcol = 2 * ATT_WIDTH // width
    return pl.pallas_call(
        functools.partial(_attn_body, tq=tq),
        grid=(N_HEADS // hs, t_real // tq),
        in_specs=[
            pl.BlockSpec((tq, width), lambda g, i: (i, g)),
            pl.BlockSpec((hs, tq, LANES), lambda g, i: (g, i, 0)),
            pl.BlockSpec((t_real, width), lambda g, i: (0, kcol + g)),
            pl.BlockSpec((hs, t_real, LANES), lambda g, i: (g, 0, 0)),
            pl.BlockSpec((t_real, width), lambda g, i: (0, vcol + g)),
            pl.BlockSpec((N_META, width), lambda g, i: (0, kcol + g)),
            pl.BlockSpec((hs, N_META, LANES), lambda g, i: (g, 0, 0)),
            pl.BlockSpec((hs, HEAD_DIM, N_META), lambda g, i: (g, 0, 0)),
        ],
        out_specs=pl.BlockSpec((tq, width), lambda g, i: (i, g)),
        out_shape=jax.ShapeDtypeStruct((t_real, ATT_WIDTH), BF16),
        scratch_shapes=[
            pltpu.VMEM((hs, t_real // tk, HEAD_DIM, tk), BF16),
            pltpu.VMEM((hs, tk, tq), F32),
            pltpu.VMEM((hs, tk, tq), F32),
            pltpu.VMEM((hs, tk, tq), BF16),
            pltpu.VMEM((hs, HEAD_DIM, tq), F32),
        ],
        compiler_params=_params(("arbitrary", "arbitrary")),
        name="fox_attention",
    )(proj, qa, proj, ka, proj, proj_meta, kam, vm_t)


CONV_HIST = 32
CONV_ROWS = 128


def _glu(a_ref, g_ref, bglu_ref):
    a = a_ref[...].astype(F32) + bglu_ref[:, :CONV_CH]
    g = g_ref[...].astype(F32) + bglu_ref[:, CONV_CH:]
    return a * jax.nn.sigmoid(g)


def _conv_body(a_ref, g_ref, am_ref, gm_ref, bglu_ref, w_ref, cb_ref, gng_ref, gnb_ref,
               o_ref, ubuf, *, tm):
    i = pl.program_id(0)

    @pl.when(i == 0)
    def _():
        ubuf[0:CONV_HIST - N_META, :] = jnp.zeros((CONV_HIST - N_META, CONV_CH), F32)
        ubuf[CONV_HIST - N_META:CONV_HIST, :] = _glu(am_ref, gm_ref, bglu_ref)

    @pl.when(i > 0)
    def _():
        ubuf[0:CONV_HIST, :] = ubuf[tm:tm + CONV_HIST, :]

    ubuf[CONV_HIST:CONV_HIST + tm, :] = _glu(a_ref, g_ref, bglu_ref)

    first = CONV_HIST - (CONV_K - 1)
    for grp in range(N_GROUPS):
        lanes = slice(grp * GROUP_CH, (grp + 1) * GROUP_CH)
        for rc in range(tm // CONV_ROWS):
            base = rc * CONV_ROWS
            acc = jnp.zeros((CONV_ROWS, GROUP_CH), F32)
            for phase in range(SUBLANES):
                slab = CONV_ROWS + (SUBLANES if phase else 0)
                part = jnp.zeros((slab, GROUP_CH), F32)
                for tap in range(CONV_K):
                    if (first + tap) % SUBLANES == phase:
                        row0 = base + (first + tap) // SUBLANES * SUBLANES
                        part = part + ubuf[row0:row0 + slab, lanes] * w_ref[tap:tap + 1, lanes]
                acc = acc + part[phase:phase + CONV_ROWS, :]
            acc = acc + cb_ref[:, lanes]
            mu = jnp.mean(acc, axis=-1, keepdims=True)
            dlt = acc - mu
            var = jnp.mean(dlt * dlt, axis=-1, keepdims=True)
            y = dlt * lax.rsqrt(var + NORM_EPS) * gng_ref[:, lanes] + gnb_ref[:, lanes]
            o_ref[base:base + CONV_ROWS, lanes] = (y * jax.nn.sigmoid(y)).astype(o_ref.dtype)


def _conv_mixer(proj, proj_meta, b_glu, conv_w, conv_b, gn_g, gn_b, tm):
    t_real = proj.shape[0]
    acol = 3 * ATT_WIDTH // CONV_CH
    gcol = acol + 1
    const = lambda i: (0, 0)
    return pl.pallas_call(
        functools.partial(_conv_body, tm=tm),
        grid=(t_real // tm,),
        in_specs=[
            pl.BlockSpec((tm, CONV_CH), lambda i: (i, acol)),
            pl.BlockSpec((tm, CONV_CH), lambda i: (i, gcol)),
            pl.BlockSpec((N_META, CONV_CH), lambda i: (0, acol)),
            pl.BlockSpec((N_META, CONV_CH), lambda i: (0, gcol)),
            pl.BlockSpec((1, 2 * CONV_CH), const),
            pl.BlockSpec((CONV_K, CONV_CH), const),
            pl.BlockSpec((1, CONV_CH), const),
            pl.BlockSpec((1, CONV_CH), const),
            pl.BlockSpec((1, CONV_CH), const),
        ],
        out_specs=pl.BlockSpec((tm, CONV_CH), lambda i: (i, 0)),
        out_shape=jax.ShapeDtypeStruct((t_real, CONV_CH), BF16),
        scratch_shapes=[pltpu.VMEM((tm + CONV_HIST, CONV_CH), F32)],
        compiler_params=_params(("arbitrary",)),
        name="conv_mixer",
    )(proj, proj, proj_meta, proj_meta, b_glu, conv_w, conv_b, gn_g, gn_b)


def _pack_bf16_pairs(hi_half, lo_half):
    lo = lax.bitcast_convert_type(lo_half.astype(F32), U32) >> 16
    hi = lax.bitcast_convert_type(hi_half.astype(F32), U32) & jnp.uint32(HI_MASK)
    return lo | hi


def _unpack_bf16_pairs(words):
    lo = lax.bitcast_convert_type(words << 16, F32).astype(BF16)
    hi = lax.bitcast_convert_type(words & jnp.uint32(HI_MASK), F32).astype(BF16)
    return jnp.concatenate([lo, hi], axis=1)


def _outproj_body(att_ref, conv_ref, x_ref, w_ref, g_ref, rwh_ref, rwl_ref, rb_ref,
                  h_ref, hnp_ref, topi_ref, gate_ref):
    mix = jnp.dot(att_ref[...], w_ref[0:ATT_WIDTH, :], preferred_element_type=F32)
    mix = mix + jnp.dot(conv_ref[...], w_ref[ATT_WIDTH:, :], preferred_element_type=F32)
    h = x_ref[...] + mix
    h_ref[...] = h
    inv = lax.rsqrt(jnp.mean(h * h, axis=-1, keepdims=True) + NORM_EPS)
    hn = h * inv * g_ref[...]
    hn_hi = hn.astype(BF16)
    hnp_ref[...] = _pack_bf16_pairs(hn_hi[:, HALF:], hn_hi[:, :HALF])

    hn_lo = (hn - hn_hi.astype(F32)).astype(BF16)
    logits = jnp.dot(hn_hi, rwh_ref[...], preferred_element_type=F32)
    logits = logits + jnp.dot(hn_hi, rwl_ref[...], preferred_element_type=F32)
    logits = logits + jnp.dot(hn_lo, rwh_ref[...], preferred_element_type=F32) + rb_ref[...]
    lane = lax.broadcasted_iota(jnp.int32, logits.shape, 1)
    vals = logits
    tops = []
    idxs = []
    for _ in range(TOP_K):
        mx = jnp.max(vals, axis=-1, keepdims=True)
        ix = jnp.min(jnp.where(vals == mx, lane, LANES), axis=-1, keepdims=True)
        tops.append(mx)
        idxs.append(ix)
        vals = jnp.where(lane == ix, -jnp.inf, vals)
    exps = [jnp.exp(v - tops[0]) for v in tops]
    denom = exps[0] + exps[1] + exps[2] + exps[3]
    topi = jnp.full(logits.shape, -1, jnp.int32)
    gate = jnp.zeros(logits.shape, F32)
    for k in range(TOP_K):
        topi = jnp.where(lane == k, idxs[k], topi)
        gate = jnp.where(lane == k, exps[k] / denom, gate)
    topi_ref[...] = topi
    gate_ref[...] = gate


def _outproj_router(att, conv, x, w_out, g, rw_hi, rw_lo, router_b, tm):
    t_real, d = x.shape
    const = lambda i: (0, 0)
    return pl.pallas_call(
        _outproj_body,
        grid=(t_real // tm,),
        in_specs=[
            pl.BlockSpec((tm, ATT_WIDTH), lambda i: (i, 0)),
            pl.BlockSpec((tm, CONV_CH), lambda i: (i, 0)),
            pl.BlockSpec((tm, d), lambda i: (i, 0)),
            pl.BlockSpec((d, d), const),
            pl.BlockSpec((1, d), const),
            pl.BlockSpec((d, LANES), const),
            pl.BlockSpec((d, LANES), const),
            pl.BlockSpec((1, LANES), const),
        ],
        out_specs=[
            pl.BlockSpec((tm, d), lambda i: (i, 0)),
            pl.BlockSpec((tm, HALF), lambda i: (i, 0)),
            pl.BlockSpec((tm, LANES), lambda i: (i, 0)),
            pl.BlockSpec((tm, LANES), lambda i: (i, 0)),
        ],
        out_shape=[
            jax.ShapeDtypeStruct((t_real, d), F32),
            jax.ShapeDtypeStruct((t_real, HALF), U32),
            jax.ShapeDtypeStruct((t_real, LANES), jnp.int32),
            jax.ShapeDtypeStruct((t_real, LANES), F32),
        ],
        compiler_params=_params(("arbitrary",)),
        name="outproj_router",
    )(att, conv, x, w_out, g, rw_hi, rw_lo, router_b)


MOE_TM = 256
PLAN_TB = 256
PLAN_ROWS = SUBLANES


def _plan_body(topi_ref, dest_ref, meta_ref, cnt_ref, carry_ref, start_ref):
    phase = pl.program_id(0)
    j = pl.program_id(1)
    topi = topi_ref[...]
    lane = lax.broadcasted_iota(jnp.int32, topi.shape, 1)
    picks = [topi[:, k:k + 1] for k in range(TOP_K)]
    onehot = jnp.zeros(topi.shape, F32)
    for pick in picks:
        onehot = onehot + (lane == pick).astype(F32)
    colsum = jnp.sum(onehot, axis=0, keepdims=True)

    @pl.when((phase == 0) & (j == 0))
    def _():
        cnt_ref[...] = jnp.zeros(cnt_ref.shape, F32)

    @pl.when(phase == 0)
    def _():
        cnt_ref[...] += colsum

    @pl.when((phase == 1) & (j == 0))
    def _():
        cnt = jnp.broadcast_to(cnt_ref[...], (SUBLANES, LANES))
        tiles = jnp.floor((cnt + (MOE_TM - 1)) * (1.0 / MOE_TM))
        padded = tiles * MOE_TM
        row = lax.broadcasted_iota(jnp.int32, (LANES, LANES), 0)
        col = lax.broadcasted_iota(jnp.int32, (LANES, LANES), 1)
        upper = (row <= col).astype(F32)
        pad_end = jnp.dot(padded, upper, preferred_element_type=F32,
                          precision=lax.Precision.HIGHEST)
        start = pad_end - padded
        start_ref[...] = start[0:1, :]
        carry_ref[...] = jnp.zeros(carry_ref.shape, F32)
        sub = lax.broadcasted_iota(jnp.int32, (PLAN_ROWS, LANES), 0)
        meta_ref[...] = jnp.where(sub == 0, cnt, jnp.where(sub == 1, start, tiles)).astype(jnp.int32)

    @pl.when(phase == 1)
    def _():
        row = lax.broadcasted_iota(jnp.int32, (PLAN_TB, PLAN_TB), 0)
        col = lax.broadcasted_iota(jnp.int32, (PLAN_TB, PLAN_TB), 1)
        earlier = (col < row).astype(BF16)
        before = jnp.dot(earlier, onehot.astype(BF16), preferred_element_type=F32)
        slot = before + carry_ref[...] + start_ref[...]
        dest = jnp.zeros(topi.shape, jnp.int32)
        for k, pick in enumerate(picks):
            d_k = jnp.sum(jnp.where(lane == pick, slot, 0.0), axis=-1, keepdims=True)
            dest = jnp.where(lane == k, d_k.astype(jnp.int32), dest)
        dest_ref[...] = dest
        carry_ref[...] += colsum


def _routing_plan(top_i):
    t_real = top_i.shape[0]
    return pl.pallas_call(
        _plan_body,
        grid=(2, t_real // PLAN_TB),
        in_specs=[pl.BlockSpec((PLAN_TB, LANES), lambda p, j: (j, 0))],
        out_specs=[
            pl.BlockSpec((PLAN_TB, LANES), lambda p, j: (j * p, 0)),
            pl.BlockSpec((PLAN_ROWS, LANES), lambda p, j: (0, 0)),
        ],
        out_shape=[
            jax.ShapeDtypeStruct((t_real, LANES), jnp.int32),
            jax.ShapeDtypeStruct((PLAN_ROWS, LANES), jnp.int32),
        ],
        scratch_shapes=[pltpu.VMEM((1, LANES), F32)] * 3,
        compiler_params=_params(("arbitrary", "arbitrary")),
        name="routing_plan",
    )(top_i)


MOE_ITEM_TILES = 5
MOE_ITEM_ROWS = MOE_ITEM_TILES * MOE_TM


def _item_tables(counts, starts, tiles_e, n_items):
    items_e = (tiles_e + MOE_ITEM_TILES - 1) // MOE_ITEM_TILES
    item_end = jnp.cumsum(items_e)
    item_start = item_end - items_e
    wid = jnp.arange(n_items, dtype=jnp.int32)
    e_of = jnp.minimum(jnp.sum((item_end[None, :] <= wid[:, None]).astype(jnp.int32), axis=1),
                       N_EXPERTS - 1)
    k_of = wid - item_start[e_of]
    live = wid < item_end[-1]
    last_e = jnp.max(jnp.where(counts > 0, jnp.arange(N_EXPERTS, dtype=jnp.int32), 0))
    item_e = jnp.where(live, e_of, last_e).astype(jnp.int32)
    item_nt = jnp.where(live, jnp.minimum(MOE_ITEM_TILES, tiles_e[e_of] - k_of * MOE_ITEM_TILES), 0)
    item_row0 = jnp.where(live, starts[e_of] + k_of * MOE_ITEM_ROWS, 0)
    return item_e, item_row0.astype(jnp.int32), item_nt.astype(jnp.int32)


DISP_TB = 256


def _dispatch_body(cnt_ref, start_ref, tiles_ref, nused_ref, dest_ref, src_ref, xs_hbm,
                   rows, zrow, ztile, sem, zsem):
    t = pl.program_id(0)
    n_steps = pl.num_programs(0)
    n_tiles = xs_hbm.shape[0] // MOE_TM
    pushes = DISP_TB * TOP_K
    slot = t % 2
    rows[slot] = src_ref[...]

    def issue(r, carry):
        for k in range(TOP_K):
            pltpu.make_async_copy(
                rows.at[slot, pl.ds(r, 1), :],
                xs_hbm.at[pl.ds(dest_ref[0, r * TOP_K + k], 1), :],
                sem.at[slot]).start(priority=k % 2)
        return carry

    lax.fori_loop(0, DISP_TB, issue, 0)

    def retire_step(buf):
        pltpu.make_async_copy(xs_hbm.at[pl.ds(0, pushes), :], xs_hbm.at[pl.ds(0, pushes), :],
                              sem.at[buf]).wait()

    @pl.when(t > 0)
    def _():
        retire_step(1 - slot)

    @pl.when(t == n_steps - 1)
    def _():
        retire_step(slot)
        zrow[...] = jnp.zeros(zrow.shape, zrow.dtype)
        ztile[...] = jnp.zeros(ztile.shape, ztile.dtype)

        def zero_row(r):
            return pltpu.make_async_copy(zrow, xs_hbm.at[pl.ds(r, 1), :], zsem)

        def per_expert(e, carry):
            lo = start_ref[e] + cnt_ref[e]
            hi = start_ref[e] + tiles_ref[e] * MOE_TM
            lax.fori_loop(lo, hi, lambda r, c: (zero_row(r).start(), c)[1], 0)
            lax.fori_loop(lo, hi, lambda r, c: (zero_row(r).wait(), c)[1], 0)
            return carry

        lax.fori_loop(0, N_EXPERTS, per_expert, 0)

        def zero_tile(tix, carry):
            cp = pltpu.make_async_copy(
                ztile, xs_hbm.at[pl.ds(pl.multiple_of(tix * MOE_TM, MOE_TM), MOE_TM), :], zsem)
            cp.start()
            cp.wait()
            return carry

        lax.fori_loop(nused_ref[0], n_tiles, zero_tile, 0)


def _dispatch(counts, starts, tiles_e, n_used, dest4, hn_packed, n_tiles):
    t_real = hn_packed.shape[0]
    n_steps = t_real // DISP_TB
    grid_spec = pltpu.PrefetchScalarGridSpec(
        num_scalar_prefetch=4,
        grid=(n_steps,),
        in_specs=[
            pl.BlockSpec((None, 1, DISP_TB * TOP_K), lambda t, *_: (t, 0, 0),
                         memory_space=pltpu.SMEM),
            pl.BlockSpec((DISP_TB, HALF), lambda t, *_: (t, 0)),
        ],
        out_specs=pl.BlockSpec(memory_space=pl.ANY),
        scratch_shapes=[
            pltpu.VMEM((2, DISP_TB, HALF), U32),
            pltpu.VMEM((1, HALF), U32),
            pltpu.VMEM((MOE_TM, HALF), U32),
            pltpu.SemaphoreType.DMA((2,)),
            pltpu.SemaphoreType.DMA(()),
        ],
    )
    return pl.pallas_call(
        _dispatch_body,
        grid_spec=grid_spec,
        out_shape=jax.ShapeDtypeStruct((n_tiles * MOE_TM, HALF), U32),
        compiler_params=_params(("arbitrary",)),
        name="moe_dispatch",
    )(counts, starts, tiles_e, n_used, dest4.reshape(n_steps, 1, DISP_TB * TOP_K), hn_packed)


MOE_TF = 512
MOE_NF = D_FF // MOE_TF


def _moe_body(e_ref, row0_ref, nt_ref, nused_ref, xs_hbm, wg_ref, wu_ref, bg_ref, bu_ref,
              wd_ref, bd_ref, y_hbm, xq, yacc, sem_x, sem_y):
    w = pl.program_id(0)
    f = pl.program_id(1)
    n_items = pl.num_programs(0)
    nt = nt_ref[w]
    slot = w % 2

    def x_copy(item, r, buf):
        return pltpu.make_async_copy(
            xs_hbm.at[pl.ds(pl.multiple_of(row0_ref[item] + r * MOE_TM, MOE_TM), MOE_TM), :],
            xq.at[buf, pl.ds(r * MOE_TM, MOE_TM), :], sem_x.at[buf])

    def y_copy(r):
        return pltpu.make_async_copy(
            yacc.at[pl.ds(pl.multiple_of(r * MOE_TM, MOE_TM), MOE_TM), :],
            y_hbm.at[pl.ds(pl.multiple_of(row0_ref[w] + r * MOE_TM, MOE_TM), MOE_TM), :], sem_y)

    def for_tiles(item, fn):
        for r in range(MOE_ITEM_TILES):
            @pl.when(r < nt_ref[item])
            def _():
                fn(r)

    @pl.when(f == 0)
    def _():
        @pl.when(w == 0)
        def _():
            for_tiles(0, lambda r: x_copy(0, r, 0).start())

        for_tiles(w, lambda r: x_copy(w, r, slot).wait())

        @pl.when(w + 1 < n_items)
        def _():
            nxt = jnp.minimum(w + 1, n_items - 1)
            for_tiles(nxt, lambda r: x_copy(nxt, r, 1 - slot).start())

    @pl.when(nt > 0)
    def _():
        def tile_rows(r):
            return pl.ds(pl.multiple_of(r * MOE_TM, MOE_TM), MOE_TM)

        def hidden(r):
            x = _unpack_bf16_pairs(xq[slot, tile_rows(r), :])
            gate = jnp.dot(x, wg_ref[...].astype(BF16), preferred_element_type=F32) + bg_ref[...]
            up = jnp.dot(x, wu_ref[...].astype(BF16), preferred_element_type=F32) + bu_ref[...]
            gate = jnp.minimum(gate, SWIGLU_LIMIT)
            up = jnp.clip(up, -SWIGLU_LIMIT, SWIGLU_LIMIT)
            return (gate * jax.nn.sigmoid(SWIGLU_ALPHA * gate) * (up + 1.0)).astype(BF16)

        def project(r, act):
            yacc[tile_rows(r), :] += jnp.dot(act, wd_ref[...].astype(BF16),
                                             preferred_element_type=F32)

            @pl.when(f == MOE_NF - 1)
            def _():
                y_copy(r).start()

        def tile(r, act_prev):
            act = hidden(r)
            project(r - 1, act_prev)
            return act

        @pl.when(f == 0)
        def _():
            def seed(r, carry):
                yacc[tile_rows(r), :] = jnp.broadcast_to(bd_ref[...], (MOE_TM, D_MODEL))
                return carry

            lax.fori_loop(0, nt, seed, 0)

        project(nt - 1, lax.fori_loop(1, nt, tile, hidden(0)))

        @pl.when(f == MOE_NF - 1)
        def _():
            for_tiles(w, lambda r: y_copy(r).wait())

    @pl.when((w == n_items - 1) & (f == MOE_NF - 1))
    def _():
        yacc[0:MOE_TM, :] = jnp.zeros((MOE_TM, D_MODEL), F32)

        def zero_tile(tix, carry):
            cp = pltpu.make_async_copy(
                yacc.at[pl.ds(0, MOE_TM), :],
                y_hbm.at[pl.ds(pl.multiple_of(tix * MOE_TM, MOE_TM), MOE_TM), :], sem_y)
            cp.start()
            cp.wait()
            return carry

        lax.fori_loop(nused_ref[0], y_hbm.shape[0] // MOE_TM, zero_tile, 0)


def _moe_ffn(item_e, item_row0, item_nt, n_used, xs, w_gate_up, b_gate_up, w_down, b_down, n_items):
    n_rows = xs.shape[0]

    def fidx(w, f, nt):
        return jnp.where(nt[w] > 0, f, MOE_NF - 1)

    grid_spec = pltpu.PrefetchScalarGridSpec(
        num_scalar_prefetch=4,
        grid=(n_items, MOE_NF),
        in_specs=[
            pl.BlockSpec(memory_space=pl.ANY),
            pl.BlockSpec((None, D_MODEL, MOE_TF),
                         lambda w, f, e, r0, nt, nu: (e[w], 0, fidx(w, f, nt))),
            pl.BlockSpec((None, D_MODEL, MOE_TF),
                         lambda w, f, e, r0, nt, nu: (e[w], 0, MOE_NF + fidx(w, f, nt))),
            pl.BlockSpec((None, 1, MOE_TF), lambda w, f, e, r0, nt, nu: (e[w], 0, fidx(w, f, nt))),
            pl.BlockSpec((None, 1, MOE_TF),
                         lambda w, f, e, r0, nt, nu: (e[w], 0, MOE_NF + fidx(w, f, nt))),
            pl.BlockSpec((None, MOE_TF, D_MODEL),
                         lambda w, f, e, r0, nt, nu: (e[w], fidx(w, f, nt), 0)),
            pl.BlockSpec((None, 1, D_MODEL), lambda w, f, e, r0, nt, nu: (e[w], 0, 0)),
        ],
        out_specs=pl.BlockSpec(memory_space=pl.ANY),
        scratch_shapes=[
            pltpu.VMEM((2, MOE_ITEM_ROWS, HALF), U32),
            pltpu.VMEM((MOE_ITEM_ROWS, D_MODEL), F32),
            pltpu.SemaphoreType.DMA((2,)),
            pltpu.SemaphoreType.DMA(()),
        ],
    )
    return pl.pallas_call(
        _moe_body,
        grid_spec=grid_spec,
        out_shape=jax.ShapeDtypeStruct((n_rows, D_MODEL), F32),
        compiler_params=_params(("arbitrary", "arbitrary")),
        name="moe_ffn",
    )(item_e, item_row0, item_nt, n_used, xs, w_gate_up, w_gate_up,
      b_gate_up.reshape(N_EXPERTS, 1, 2 * D_FF), b_gate_up.reshape(N_EXPERTS, 1, 2 * D_FF),
      w_down, b_down.reshape(N_EXPERTS, 1, D_MODEL))


COMB_TM = 128


def _combine_body(dest_ref, dnext_ref, y_hbm, gate_ref, h_ref, g_ref, o_ref, stage, sem):
    t = pl.program_id(0)
    n_steps = pl.num_programs(0)
    slot = t % 2

    def pull(idx_ref, r, k, buf):
        return pltpu.make_async_copy(
            y_hbm.at[pl.ds(idx_ref[0, r * TOP_K + k], 1), :],
            stage.at[buf, pl.ds(k * COMB_TM + r, 1), :], sem.at[buf])

    def issue_all(idx_ref, buf):
        def issue(r, carry):
            for k in range(TOP_K):
                pull(idx_ref, r, k, buf).start(priority=k % 2)
            return carry

        lax.fori_loop(0, COMB_TM, issue, 0)

    @pl.when(t == 0)
    def _():
        issue_all(dest_ref, 0)

    @pl.when(t + 1 < n_steps)
    def _():
        issue_all(dnext_ref, 1 - slot)

    pltpu.make_async_copy(y_hbm.at[pl.ds(0, TOP_K * COMB_TM), :], stage.at[slot],
                          sem.at[slot]).wait()

    gates = gate_ref[...]
    ff = gates[:, 0:1] * stage[slot, 0:COMB_TM, :]
    for k in range(1, TOP_K):
        ff = ff + gates[:, k:k + 1] * stage[slot, k * COMB_TM:(k + 1) * COMB_TM, :]
    h = h_ref[...] + ff
    inv = lax.rsqrt(jnp.mean(h * h, axis=-1, keepdims=True) + NORM_EPS)
    o_ref[...] = h * inv * g_ref[...]


def _combine(dest4, y_rows, gates, h1, g):
    t_real, d = h1.shape
    n_steps = t_real // COMB_TM
    dest_blocks = dest4.reshape(n_steps, 1, COMB_TM * TOP_K)
    smem_block = (None, 1, COMB_TM * TOP_K)
    return pl.pallas_call(
        _combine_body,
        grid=(n_steps,),
        in_specs=[
            pl.BlockSpec(smem_block, lambda i: (i, 0, 0), memory_space=pltpu.SMEM),
            pl.BlockSpec(smem_block, lambda i: (jnp.minimum(i + 1, n_steps - 1), 0, 0),
                         memory_space=pltpu.SMEM),
            pl.BlockSpec(memory_space=pl.ANY),
            pl.BlockSpec((COMB_TM, LANES), lambda i: (i, 0)),
            pl.BlockSpec((COMB_TM, d), lambda i: (i, 0)),
            pl.BlockSpec((1, d), lambda i: (0, 0)),
        ],
        out_specs=pl.BlockSpec((COMB_TM, d), lambda i: (i, 0)),
        out_shape=jax.ShapeDtypeStruct((t_real, d), F32),
        scratch_shapes=[
            pltpu.VMEM((2, TOP_K * COMB_TM, d), F32),
            pltpu.SemaphoreType.DMA((2,)),
        ],
        compiler_params=_params(("arbitrary",)),
        name="moe_combine",
    )(dest_blocks, dest_blocks, y_rows, gates, h1, g)


def kernel(x, meta_tokens, norm_mix_g, w_in, b_fgate, b_glu, conv_w, conv_b, gn_g, gn_b, w_out,
           norm_ffn_g, router_w, router_b, w_gate_up, b_gate_up, w_down, b_down, norm_final_g):
    assert x.shape[0] == 1 and norm_mix_g.shape[0] == 1
    xt = x[0]
    t_real = xt.shape[0]
    fcol = 3 * ATT_WIDTH
    w_in0 = w_in[0]
    w_main = jnp.concatenate([w_in0[:, :fcol], w_in0[:, fcol + N_HEADS:]], axis=1).astype(BF16)
    w_f = jnp.pad(w_in0[:, fcol:fcol + N_HEADS], ((0, 0), (0, LANES - N_HEADS))).astype(BF16)
    g_mix = norm_mix_g[0].reshape(1, D_MODEL)

    proj, f_real = _inproj(xt, g_mix, w_main, w_f, tm=1024, tn=1024)
    proj_meta, f_meta = _inproj(meta_tokens, g_mix, w_main, w_f, tm=N_META, tn=512)

    b_f = jnp.pad(b_fgate[0], (0, LANES - N_HEADS)).reshape(1, LANES)
    qa, ka, kam = _fgate_bias(f_meta, f_real, b_f)
    v_meta = proj_meta[:, 2 * ATT_WIDTH:3 * ATT_WIDTH]
    vm_t = v_meta.reshape(N_META, N_HEADS, HEAD_DIM).transpose(1, 2, 0)
    att = _attention(proj, proj_meta, vm_t, qa, ka, kam, tq=512, tk=256)
    conv = _conv_mixer(proj, proj_meta, b_glu[0].reshape(1, -1), conv_w[0],
                       conv_b[0].reshape(1, -1), gn_g[0].reshape(1, -1), gn_b[0].reshape(1, -1),
                       tm=512)

    rw = jnp.pad(router_w[0], ((0, 0), (0, LANES - N_EXPERTS)))
    rw_hi = rw.astype(BF16)
    rw_lo = (rw - rw_hi.astype(F32)).astype(BF16)
    rb = jnp.pad(router_b[0], (0, LANES - N_EXPERTS), constant_values=NEG_BIG).reshape(1, LANES)
    h1, hn_packed, top_i, gates = _outproj_router(
        att, conv, xt, w_out[0].astype(BF16), norm_ffn_g[0].reshape(1, D_MODEL),
        rw_hi, rw_lo, rb, tm=256)

    n_assign = t_real * TOP_K
    n_tiles = (n_assign + N_EXPERTS * (MOE_TM - 1)) // MOE_TM + 1
    n_items = N_EXPERTS + (n_tiles * MOE_TM) // MOE_ITEM_ROWS
    dest, plan = _routing_plan(top_i)
    counts, starts, tiles_e = (plan[r, :N_EXPERTS] for r in range(3))
    n_used = jnp.sum(tiles_e).astype(jnp.int32).reshape(1)
    item_e, item_row0, item_nt = _item_tables(counts, starts, tiles_e, n_items)
    dest4 = dest[:, :TOP_K]

    xs = _dispatch(counts, starts, tiles_e, n_used, dest4, hn_packed, n_tiles)
    y_rows = _moe_ffn(item_e, item_row0, item_nt, n_used, xs, w_gate_up[0], b_gate_up[0],
                      w_down[0], b_down[0], n_items)
    out = _combine(dest4, y_rows, gates, h1, norm_final_g.reshape(1, D_MODEL))
    return out[None]
```

```python
import functools

import jax
import jax.numpy as jnp
import numpy as np
from jax import lax
from jax.experimental import pallas as pl
from jax.experimental.pallas import tpu as pltpu

D_MODEL = 2048
N_META = 16
HEAD_DIM = 128
ATT_WIDTH = 1024
N_HEADS = ATT_WIDTH // HEAD_DIM
CONV_CH = 1024
N_GROUPS = 8
GROUP_CH = CONV_CH // N_GROUPS
CONV_K = 31
N_EXPERTS = 32
TOP_K = 4
D_FF = 2048
SWIGLU_LIMIT = 7.0
SWIGLU_ALPHA = 1.702
NORM_EPS = 1e-5
NEG_BIG = -1e30

LANES = 128
SUBLANES = 8
VMEM_LIMIT = 56 * 1024 * 1024
HALF = D_MODEL // 2

F32 = jnp.float32
BF16 = jnp.bfloat16
U32 = jnp.uint32
HI_MASK = 0xFFFF0000


def _params(sem, vmem=VMEM_LIMIT):
    return pltpu.CompilerParams(dimension_semantics=sem, vmem_limit_bytes=vmem)


def _inproj_body(x_ref, g_ref, wa_ref, wc_ref, wf_ref, o_ref, f_ref, hn_ref, *, n_att):
    j = pl.program_id(1)

    @pl.when(j == 0)
    def _():
        x = x_ref[...]
        inv = lax.rsqrt(jnp.mean(x * x, axis=-1, keepdims=True) + NORM_EPS)
        hn = (x * inv * g_ref[...]).astype(BF16)
        hn_ref[...] = hn
        f_ref[...] = jnp.dot(hn, wf_ref[...], preferred_element_type=F32)

    @pl.when(j < n_att)
    def _():
        o_ref[...] = jnp.dot(hn_ref[...], wa_ref[...],
                             preferred_element_type=F32).astype(o_ref.dtype)

    @pl.when(j >= n_att)
    def _():
        o_ref[...] = jnp.dot(hn_ref[...], wc_ref[...],
                             preferred_element_type=F32).astype(o_ref.dtype)


def _inproj(x, g, w_all, w_glu, w_f, tm, tn):
    m, d = x.shape
    n_att = 3 * ATT_WIDTH // tn
    n = 3 * ATT_WIDTH + w_glu.shape[1]
    return pl.pallas_call(
        functools.partial(_inproj_body, n_att=n_att),
        grid=(m // tm, n // tn),
        in_specs=[
            pl.BlockSpec((tm, d), lambda i, j: (i, 0)),
            pl.BlockSpec((1, d), lambda i, j: (0, 0)),
            pl.BlockSpec((d, tn), lambda i, j: (0, jnp.minimum(j, n_att - 1))),
            pl.BlockSpec((d, tn), lambda i, j: (0, jnp.maximum(j - n_att, 0))),
            pl.BlockSpec((d, LANES), lambda i, j: (0, 0)),
        ],
        out_specs=[
            pl.BlockSpec((tm, tn), lambda i, j: (i, j)),
            pl.BlockSpec((tm, LANES), lambda i, j: (i, 0)),
        ],
        out_shape=[
            jax.ShapeDtypeStruct((m, n), BF16),
            jax.ShapeDtypeStruct((m, LANES), F32),
        ],
        scratch_shapes=[pltpu.VMEM((tm, d), BF16)],
        compiler_params=_params(("arbitrary", "arbitrary")),
        name="inproj",
    )(x, g, w_all, w_glu, w_f)


ATT_SCALE = 1.0 / float(np.sqrt(HEAD_DIM))
EXP2_MULT = ATT_SCALE * float(np.log2(np.e))
N_SPLIT = 3
FGATE_TB = 512


def _log_sigmoid(x):
    return jnp.minimum(x, 0.0) - jnp.log1p(jnp.exp(-jnp.abs(x)))


def _prefix_rows(lf):
    n = lf.shape[0]
    row = lax.broadcasted_iota(jnp.int32, (n, n), 0)
    col = lax.broadcasted_iota(jnp.int32, (n, n), 1)
    tri = (col <= row).astype(F32)
    return jnp.dot(tri, lf, preferred_element_type=F32, precision=lax.Precision.HIGHEST)


def _bias_columns(c, head):
    lane = lax.broadcasted_iota(jnp.int32, c.shape, 1)
    x = jnp.sum(jnp.where(lane == head, c, 0.0), axis=-1, keepdims=True) * float(np.sqrt(HEAD_DIM))
    hi = x.astype(BF16).astype(F32)
    r1 = x - hi
    mid = r1.astype(BF16).astype(F32)
    lo = r1 - mid
    part = lane % N_SPLIT
    parts = jnp.where(part == 0, hi, jnp.where(part == 1, mid, lo))
    qa = jnp.where(lane < N_SPLIT, parts, jnp.where(lane < 2 * N_SPLIT, 1.0, 0.0))
    ka = jnp.where(lane < N_SPLIT, 1.0, jnp.where(lane < 2 * N_SPLIT, -parts, 0.0))
    return qa.astype(BF16), ka.astype(BF16)


def _fgate_body(fm_ref, f_ref, b_ref, qa_ref, ka_ref, kam_ref, carry_ref):
    @pl.when(pl.program_id(0) == 0)
    def _():
        cm = _prefix_rows(_log_sigmoid(fm_ref[...] + b_ref[...]))
        carry_ref[...] = cm[N_META - 1:N_META, :]
        for head in range(N_HEADS):
            kam_ref[head] = _bias_columns(cm, head)[1]

    c = _prefix_rows(_log_sigmoid(f_ref[...] + b_ref[...])) + carry_ref[...]
    carry_ref[...] = c[FGATE_TB - 1:FGATE_TB, :]
    for head in range(N_HEADS):
        qa, ka = _bias_columns(c, head)
        qa_ref[head] = qa
        ka_ref[head] = ka


def _fgate_bias(f_meta, f_real, b_f):
    t_real = f_real.shape[0]
    return pl.pallas_call(
        _fgate_body,
        grid=(t_real // FGATE_TB,),
        in_specs=[
            pl.BlockSpec((N_META, LANES), lambda i: (0, 0)),
            pl.BlockSpec((FGATE_TB, LANES), lambda i: (i, 0)),
            pl.BlockSpec((1, LANES), lambda i: (0, 0)),
        ],
        out_specs=[
            pl.BlockSpec((N_HEADS, FGATE_TB, LANES), lambda i: (0, i, 0)),
            pl.BlockSpec((N_HEADS, FGATE_TB, LANES), lambda i: (0, i, 0)),
            pl.BlockSpec((N_HEADS, N_META, LANES), lambda i: (0, 0, 0)),
        ],
        out_shape=[
            jax.ShapeDtypeStruct((N_HEADS, t_real, LANES), BF16),
            jax.ShapeDtypeStruct((N_HEADS, t_real, LANES), BF16),
            jax.ShapeDtypeStruct((N_HEADS, N_META, LANES), BF16),
        ],
        scratch_shapes=[pltpu.VMEM((1, LANES), F32)],
        compiler_params=_params(("arbitrary",)),
        name="fgate_bias",
    )(f_meta, f_real, b_f)


ATT_PACK_ROWS = 16
ATT_HEADS_PER_STEP = 2


def _attn_body(q_ref, qa_ref, k_ref, ka_ref, v_ref, km_ref, kam_ref, vmt_ref, o_ref,
               vt_ref, s_a, s_b, p_ref, acc_ref, *, tq):
    i = pl.program_id(1)
    heads, n_blocks, _, tk = vt_ref.shape
    assert tq == 2 * tk
    contract_last = (((1,), (1,)), ((), ()))

    def cols(hh):
        return slice(hh * HEAD_DIM, (hh + 1) * HEAD_DIM)

    @pl.when(i == 0)
    def _():
        for hh in range(heads):
            for blk in range(n_blocks):
                vt_ref[hh, blk] = v_ref[blk * tk:(blk + 1) * tk, cols(hh)].T

    q = [jnp.concatenate([q_ref[:, cols(hh)], qa_ref[hh]], axis=1) for hh in range(heads)]

    def scores_t(hh, k, ka):
        return lax.dot_general(jnp.concatenate([k, ka], axis=1), q[hh], contract_last,
                               preferred_element_type=F32)

    def block_scores(hh, j):
        rows = pl.ds(pl.multiple_of(j * tk, tk), tk)
        return scores_t(hh, k_ref[rows, cols(hh)], ka_ref[hh, rows, :])

    stats = []
    for hh in range(heads):
        s = scores_t(hh, km_ref[:, cols(hh)], kam_ref[hh])
        m = jnp.max(s, axis=0, keepdims=True)
        p = jnp.exp2((s - m) * EXP2_MULT)
        stats += [m, jnp.sum(p, axis=0, keepdims=True)]
        acc_ref[hh] = jnp.dot(vmt_ref[hh], p.astype(BF16), preferred_element_type=F32)

    def absorb(hh, s_ref, j, m_prev, l_prev):
        top = s_ref[hh, 0:SUBLANES, :]
        for r in range(SUBLANES, tk, SUBLANES):
            top = jnp.maximum(top, s_ref[hh, r:r + SUBLANES, :])
        m_new = jnp.maximum(m_prev, jnp.max(top, axis=0, keepdims=True))
        alpha = jnp.exp2((m_prev - m_new) * EXP2_MULT)
        part = jnp.zeros((SUBLANES, tq), F32)
        for r in range(0, tk, ATT_PACK_ROWS):
            p = jnp.exp2((s_ref[hh, r:r + ATT_PACK_ROWS, :] - m_new) * EXP2_MULT)
            part = part + p[0:SUBLANES, :] + p[SUBLANES:, :]
            p_ref[hh, r:r + ATT_PACK_ROWS, :] = p.astype(BF16)
        l_new = alpha * l_prev + jnp.sum(part, axis=0, keepdims=True)
        acc_ref[hh] = alpha * acc_ref[hh] + jnp.dot(vt_ref[hh, j], p_ref[hh],
                                                    preferred_element_type=F32)
        return [m_new, l_new]

    def absorb_all(s_ref, j, st):
        out = []
        for hh in range(heads):
            out += absorb(hh, s_ref, j, st[2 * hh], st[2 * hh + 1])
        return out

    def pair(pi, st):
        for hh in range(heads):
            s_b[hh] = block_scores(hh, 2 * pi + 1)
        st = absorb_all(s_a, 2 * pi, st)
        for hh in range(heads):
            s_a[hh] = block_scores(hh, 2 * pi + 2)
        return tuple(absorb_all(s_b, 2 * pi + 1, st))

    for hh in range(heads):
        s_a[hh] = block_scores(hh, 0)
    stats = lax.fori_loop(0, i, pair, tuple(stats))

    key = lax.broadcasted_iota(jnp.int32, (tk, tq), 0)
    qry = lax.broadcasted_iota(jnp.int32, (tk, tq), 1)
    for hh in range(heads):
        s_b[hh] = jnp.where(key + tk <= qry, block_scores(hh, 2 * i + 1), NEG_BIG)
        s_a[hh] = jnp.where(key <= qry, s_a[hh], NEG_BIG)
    stats = absorb_all(s_b, 2 * i + 1, absorb_all(s_a, 2 * i, stats))
    for hh in range(heads):
        o_ref[:, cols(hh)] = (acc_ref[hh] / stats[2 * hh + 1]).T.astype(o_ref.dtype)


def _attention(proj, proj_meta, vm_t, qa, ka, kam, tq, tk):
    t_real = proj.shape[0]
    hs = ATT_HEADS_PER_STEP
    width = hs * HEAD_DIM
    kcol = ATT_WIDTH // width
    vcol = 2 * ATT_WIDTH // width
    return pl.pallas_call(
        functools.partial(_attn_body, tq=tq),
        grid=(N_HEADS // hs, t_real // tq),
        in_specs=[
            pl.BlockSpec((tq, width), lambda g, i: (i, g)),
            pl.BlockSpec((hs, tq, LANES), lambda g, i: (g, i, 0)),
            pl.BlockSpec((t_real, width), lambda g, i: (0, kcol + g)),
            pl.BlockSpec((hs, t_real, LANES), lambda g, i: (g, 0, 0)),
            pl.BlockSpec((t_real, width), lambda g, i: (0, vcol + g)),
            pl.BlockSpec((N_META, width), lambda g, i: (0, kcol + g)),
            pl.BlockSpec((hs, N_META, LANES), lambda g, i: (g, 0, 0)),
            pl.BlockSpec((hs, HEAD_DIM, N_META), lambda g, i: (g, 0, 0)),
        ],
        out_specs=pl.BlockSpec((tq, width), lambda g, i: (i, g)),
        out_shape=jax.ShapeDtypeStruct((t_real, ATT_WIDTH), BF16),
        scratch_shapes=[
            pltpu.VMEM((hs, t_real // tk, HEAD_DIM, tk), BF16),
            pltpu.VMEM((hs, tk, tq), F32),
            pltpu.VMEM((hs, tk, tq), F32),
            pltpu.VMEM((hs, tk, tq), BF16),
            pltpu.VMEM((hs, HEAD_DIM, tq), F32),
        ],
        compiler_params=_params(("arbitrary", "arbitrary")),
        name="fox_attention",
    )(proj, qa, proj, ka, proj, proj_meta, kam, vm_t)


CONV_HIST = 32
CONV_ROWS = 128


def _glu(a_ref, g_ref, bglu_ref):
    a = a_ref[...].astype(F32) + bglu_ref[:, :CONV_CH]
    g = g_ref[...].astype(F32) + bglu_ref[:, CONV_CH:]
    return a * jax.nn.sigmoid(g)


def _conv_body(a_ref, g_ref, am_ref, gm_ref, bglu_ref, w_ref, cb_ref, gng_ref, gnb_ref,
               o_ref, ubuf, *, tm):
    i = pl.program_id(0)

    @pl.when(i == 0)
    def _():
        ubuf[0:CONV_HIST - N_META, :] = jnp.zeros((CONV_HIST - N_META, CONV_CH), F32)
        ubuf[CONV_HIST - N_META:CONV_HIST, :] = _glu(am_ref, gm_ref, bglu_ref)

    @pl.when(i > 0)
    def _():
        ubuf[0:CONV_HIST, :] = ubuf[tm:tm + CONV_HIST, :]

    ubuf[CONV_HIST:CONV_HIST + tm, :] = _glu(a_ref, g_ref, bglu_ref)

    first = CONV_HIST - (CONV_K - 1)
    for grp in range(N_GROUPS):
        lanes = slice(grp * GROUP_CH, (grp + 1) * GROUP_CH)
        for rc in range(tm // CONV_ROWS):
            base = rc * CONV_ROWS
            acc = jnp.zeros((CONV_ROWS, GROUP_CH), F32)
            for phase in range(SUBLANES):
                slab = CONV_ROWS + (SUBLANES if phase else 0)
                part = jnp.zeros((slab, GROUP_CH), F32)
                for tap in range(CONV_K):
                    if (first + tap) % SUBLANES == phase:
                        row0 = base + (first + tap) // SUBLANES * SUBLANES
                        part = part + ubuf[row0:row0 + slab, lanes] * w_ref[tap:tap + 1, lanes]
                acc = acc + part[phase:phase + CONV_ROWS, :]
            acc = acc + cb_ref[:, lanes]
            mu = jnp.mean(acc, axis=-1, keepdims=True)
            dlt = acc - mu
            var = jnp.mean(dlt * dlt, axis=-1, keepdims=True)
            y = dlt * lax.rsqrt(var + NORM_EPS) * gng_ref[:, lanes] + gnb_ref[:, lanes]
            o_ref[base:base + CONV_ROWS, lanes] = (y * jax.nn.sigmoid(y)).astype(o_ref.dtype)


def _conv_mixer(proj, proj_meta, b_glu, conv_w, conv_b, gn_g, gn_b, tm):
    t_real = proj.shape[0]
    acol = 3 * ATT_WIDTH // CONV_CH
    gcol = acol + 1
    const = lambda i: (0, 0)
    return pl.pallas_call(
        functools.partial(_conv_body, tm=tm),
        grid=(t_real // tm,),
        in_specs=[
            pl.BlockSpec((tm, CONV_CH), lambda i: (i, acol)),
            pl.BlockSpec((tm, CONV_CH), lambda i: (i, gcol)),
            pl.BlockSpec((N_META, CONV_CH), lambda i: (0, acol)),
            pl.BlockSpec((N_META, CONV_CH), lambda i: (0, gcol)),
            pl.BlockSpec((1, 2 * CONV_CH), const),
            pl.BlockSpec((CONV_K, CONV_CH), const),
            pl.BlockSpec((1, CONV_CH), const),
            pl.BlockSpec((1, CONV_CH), const),
            pl.BlockSpec((1, CONV_CH), const),
        ],
        out_specs=pl.BlockSpec((tm, CONV_CH), lambda i: (i, 0)),
        out_shape=jax.ShapeDtypeStruct((t_real, CONV_CH), BF16),
        scratch_shapes=[pltpu.VMEM((tm + CONV_HIST, CONV_CH), F32)],
        compiler_params=_params(("arbitrary",)),
        name="conv_mixer",
    )(proj, proj, proj_meta, proj_meta, b_glu, conv_w, conv_b, gn_g, gn_b)


def _pack_bf16_pairs(hi_half, lo_half):
    lo = lax.bitcast_convert_type(lo_half.astype(F32), U32) >> 16
    hi = lax.bitcast_convert_type(hi_half.astype(F32), U32) & jnp.uint32(HI_MASK)
    return lo | hi


def _unpack_bf16_pairs(words):
    lo = lax.bitcast_convert_type(words << 16, F32).astype(BF16)
    hi = lax.bitcast_convert_type(words & jnp.uint32(HI_MASK), F32).astype(BF16)
    return jnp.concatenate([lo, hi], axis=1)


def _outproj_body(att_ref, conv_ref, x_ref, w_ref, g_ref, rwh_ref, rwl_ref, rb_ref,
                  h_ref, hnp_ref, topi_ref, gate_ref):
    mix = jnp.dot(att_ref[...], w_ref[0:ATT_WIDTH, :], preferred_element_type=F32)
    mix = mix + jnp.dot(conv_ref[...], w_ref[ATT_WIDTH:, :], preferred_element_type=F32)
    h = x_ref[...] + mix
    h_ref[...] = h
    inv = lax.rsqrt(jnp.mean(h * h, axis=-1, keepdims=True) + NORM_EPS)
    hn = h * inv * g_ref[...]
    hn_hi = hn.astype(BF16)
    hnp_ref[...] = _pack_bf16_pairs(hn_hi[:, HALF:], hn_hi[:, :HALF])

    hn_lo = (hn - hn_hi.astype(F32)).astype(BF16)
    logits = jnp.dot(hn_hi, rwh_ref[...], preferred_element_type=F32)
    logits = logits + jnp.dot(hn_hi, rwl_ref[...], preferred_element_type=F32)
    logits = logits + jnp.dot(hn_lo, rwh_ref[...], preferred_element_type=F32) + rb_ref[...]
    lane = lax.broadcasted_iota(jnp.int32, logits.shape, 1)
    vals = logits
    tops = []
    idxs = []
    for _ in range(TOP_K):
        mx = jnp.max(vals, axis=-1, keepdims=True)
        ix = jnp.min(jnp.where(vals == mx, lane, LANES), axis=-1, keepdims=True)
        tops.append(mx)
        idxs.append(ix)
        vals = jnp.where(lane == ix, -jnp.inf, vals)
    exps = [jnp.exp(v - tops[0]) for v in tops]
    denom = exps[0] + exps[1] + exps[2] + exps[3]
    topi = jnp.full(logits.shape, -1, jnp.int32)
    gate = jnp.zeros(logits.shape, F32)
    for k in range(TOP_K):
        topi = jnp.where(lane == k, idxs[k], topi)
        gate = jnp.where(lane == k, exps[k] / denom, gate)
    topi_ref[...] = topi
    gate_ref[...] = gate


def _outproj_router(att, conv, x, w_out, g, rw_hi, rw_lo, router_b, tm):
    t_real, d = x.shape
    const = lambda i: (0, 0)
    return pl.pallas_call(
        _outproj_body,
        grid=(t_real // tm,),
        in_specs=[
            pl.BlockSpec((tm, ATT_WIDTH), lambda i: (i, 0)),
            pl.BlockSpec((tm, CONV_CH), lambda i: (i, 0)),
            pl.BlockSpec((tm, d), lambda i: (i, 0)),
            pl.BlockSpec((d, d), const),
            pl.BlockSpec((1, d), const),
            pl.BlockSpec((d, LANES), const),
            pl.BlockSpec((d, LANES), const),
            pl.BlockSpec((1, LANES), const),
        ],
        out_specs=[
            pl.BlockSpec((tm, d), lambda i: (i, 0)),
            pl.BlockSpec((tm, HALF), lambda i: (i, 0)),
            pl.BlockSpec((tm, LANES), lambda i: (i, 0)),
            pl.BlockSpec((tm, LANES), lambda i: (i, 0)),
        ],
        out_shape=[
            jax.ShapeDtypeStruct((t_real, d), F32),
            jax.ShapeDtypeStruct((t_real, HALF), U32),
            jax.ShapeDtypeStruct((t_real, LANES), jnp.int32),
            jax.ShapeDtypeStruct((t_real, LANES), F32),
        ],
        compiler_params=_params(("arbitrary",)),
        name="outproj_router",
    )(att, conv, x, w_out, g, rw_hi, rw_lo, router_b)


MOE_TM = 256
PLAN_TB = 256
PLAN_ROWS = SUBLANES


def _plan_body(topi_ref, dest_ref, meta_ref, cnt_ref, carry_ref, start_ref):
    phase = pl.program_id(0)
    j = pl.program_id(1)
    topi = topi_ref[...]
    lane = lax.broadcasted_iota(jnp.int32, topi.shape, 1)
    picks = [topi[:, k:k + 1] for k in range(TOP_K)]
    onehot = jnp.zeros(topi.shape, F32)
    for pick in picks:
        onehot = onehot + (lane == pick).astype(F32)
    colsum = jnp.sum(onehot, axis=0, keepdims=True)

    @pl.when((phase == 0) & (j == 0))
    def _():
        cnt_ref[...] = jnp.zeros(cnt_ref.shape, F32)

    @pl.when(phase == 0)
    def _():
        cnt_ref[...] += colsum

    @pl.when((phase == 1) & (j == 0))
    def _():
        cnt = jnp.broadcast_to(cnt_ref[...], (SUBLANES, LANES))
        tiles = jnp.floor((cnt + (MOE_TM - 1)) * (1.0 / MOE_TM))
        padded = tiles * MOE_TM
        row = lax.broadcasted_iota(jnp.int32, (LANES, LANES), 0)
        col = lax.broadcasted_iota(jnp.int32, (LANES, LANES), 1)
        upper = (row <= col).astype(F32)
        pad_end = jnp.dot(padded, upper, preferred_element_type=F32,
                          precision=lax.Precision.HIGHEST)
        start = pad_end - padded
        start_ref[...] = start[0:1, :]
        carry_ref[...] = jnp.zeros(carry_ref.shape, F32)
        sub = lax.broadcasted_iota(jnp.int32, (PLAN_ROWS, LANES), 0)
        meta_ref[...] = jnp.where(sub == 0, cnt, jnp.where(sub == 1, start, tiles)).astype(jnp.int32)

    @pl.when(phase == 1)
    def _():
        row = lax.broadcasted_iota(jnp.int32, (PLAN_TB, PLAN_TB), 0)
        col = lax.broadcasted_iota(jnp.int32, (PLAN_TB, PLAN_TB), 1)
        earlier = (col < row).astype(BF16)
        before = jnp.dot(earlier, onehot.astype(BF16), preferred_element_type=F32)
        slot = before + carry_ref[...] + start_ref[...]
        dest = jnp.zeros(topi.shape, jnp.int32)
        for k, pick in enumerate(picks):
            d_k = jnp.sum(jnp.where(lane == pick, slot, 0.0), axis=-1, keepdims=True)
            dest = jnp.where(lane == k, d_k.astype(jnp.int32), dest)
        dest_ref[...] = dest
        carry_ref[...] += colsum


def _routing_plan(top_i):
    t_real = top_i.shape[0]
    return pl.pallas_call(
        _plan_body,
        grid=(2, t_real // PLAN_TB),
        in_specs=[pl.BlockSpec((PLAN_TB, LANES), lambda p, j: (j, 0))],
        out_specs=[
            pl.BlockSpec((PLAN_TB, LANES), lambda p, j: (j * p, 0)),
            pl.BlockSpec((PLAN_ROWS, LANES), lambda p, j: (0, 0)),
        ],
        out_shape=[
            jax.ShapeDtypeStruct((t_real, LANES), jnp.int32),
            jax.ShapeDtypeStruct((PLAN_ROWS, LANES), jnp.int32),
        ],
        scratch_shapes=[pltpu.VMEM((1, LANES), F32)] * 3,
        compiler_params=_params(("arbitrary", "arbitrary")),
        name="routing_plan",
    )(top_i)


MOE_ITEM_TILES = 5
MOE_ITEM_ROWS = MOE_ITEM_TILES * MOE_TM


def _item_tables(counts, starts, tiles_e, n_items):
    items_e = (tiles_e + MOE_ITEM_TILES - 1) // MOE_ITEM_TILES
    item_end = jnp.cumsum(items_e)
    item_start = item_end - items_e
    wid = jnp.arange(n_items, dtype=jnp.int32)
    e_of = jnp.minimum(jnp.sum((item_end[None, :] <= wid[:, None]).astype(jnp.int32), axis=1),
                       N_EXPERTS - 1)
    k_of = wid - item_start[e_of]
    live = wid < item_end[-1]
    last_e = jnp.max(jnp.where(counts > 0, jnp.arange(N_EXPERTS, dtype=jnp.int32), 0))
    item_e = jnp.where(live, e_of, last_e).astype(jnp.int32)
    item_nt = jnp.where(live, jnp.minimum(MOE_ITEM_TILES, tiles_e[e_of] - k_of * MOE_ITEM_TILES), 0)
    item_row0 = jnp.where(live, starts[e_of] + k_of * MOE_ITEM_ROWS, 0)
    return item_e, item_row0.astype(jnp.int32), item_nt.astype(jnp.int32)


DISP_TB = 256


def _dispatch_body(start_ref, tiles_ref, nused_ref, dest_ref, src_ref, xs_hbm,
                   rows, ztile, sem, zsem):
    t = pl.program_id(0)
    n_steps = pl.num_programs(0)
    n_tiles = xs_hbm.shape[0] // MOE_TM
    pushes = DISP_TB * TOP_K
    slot = t % 2
    rows[slot] = src_ref[...]

    @pl.when(t == 0)
    def _():
        ztile[...] = jnp.zeros(ztile.shape, ztile.dtype)

        def clear(tix):
            return pltpu.make_async_copy(
                ztile, xs_hbm.at[pl.ds(pl.multiple_of(tix * MOE_TM, MOE_TM), MOE_TM), :], zsem)

        def last_tile(e):
            return start_ref[e] // MOE_TM + tiles_ref[e] - 1

        def start_group(e, carry):
            @pl.when(tiles_ref[e] > 0)
            def _():
                clear(last_tile(e)).start()
            return carry

        def wait_group(e, carry):
            @pl.when(tiles_ref[e] > 0)
            def _():
                clear(last_tile(e)).wait()
            return carry

        lax.fori_loop(0, N_EXPERTS, start_group, 0)
        lax.fori_loop(nused_ref[0], n_tiles, lambda tix, c: (clear(tix).start(), c)[1], 0)
        lax.fori_loop(0, N_EXPERTS, wait_group, 0)
        lax.fori_loop(nused_ref[0], n_tiles, lambda tix, c: (clear(tix).wait(), c)[1], 0)

    def issue(r, carry):
        for k in range(TOP_K):
            pltpu.make_async_copy(
                rows.at[slot, pl.ds(r, 1), :],
                xs_hbm.at[pl.ds(dest_ref[0, r * TOP_K + k], 1), :],
                sem.at[slot]).start(priority=k % 2)
        return carry

    lax.fori_loop(0, DISP_TB, issue, 0)

    def retire_step(buf):
        pltpu.make_async_copy(xs_hbm.at[pl.ds(0, pushes), :], xs_hbm.at[pl.ds(0, pushes), :],
                              sem.at[buf]).wait()

    @pl.when(t > 0)
    def _():
        retire_step(1 - slot)

    @pl.when(t == n_steps - 1)
    def _():
        retire_step(slot)


def _dispatch(starts, tiles_e, n_used, dest4, hn_packed, n_tiles):
    t_real = hn_packed.shape[0]
    n_steps = t_real // DISP_TB
    grid_spec = pltpu.PrefetchScalarGridSpec(
        num_scalar_prefetch=3,
        grid=(n_steps,),
        in_specs=[
            pl.BlockSpec((None, 1, DISP_TB * TOP_K), lambda t, *_: (t, 0, 0),
                         memory_space=pltpu.SMEM),
            pl.BlockSpec((DISP_TB, HALF), lambda t, *_: (t, 0)),
        ],
        out_specs=pl.BlockSpec(memory_space=pl.ANY),
        scratch_shapes=[
            pltpu.VMEM((2, DISP_TB, HALF), U32),
            pltpu.VMEM((MOE_TM, HALF), U32),
            pltpu.SemaphoreType.DMA((2,)),
            pltpu.SemaphoreType.DMA(()),
        ],
    )
    return pl.pallas_call(
        _dispatch_body,
        grid_spec=grid_spec,
        out_shape=jax.ShapeDtypeStruct((n_tiles * MOE_TM, HALF), U32),
        compiler_params=_params(("arbitrary",)),
        name="moe_dispatch",
    )(starts, tiles_e, n_used, dest4.reshape(n_steps, 1, DISP_TB * TOP_K), hn_packed)


MOE_TF = 512
MOE_NF = D_FF // MOE_TF


def _moe_body(e_ref, row0_ref, nt_ref, nused_ref, xs_hbm, wg_ref, wu_ref, bg_ref, bu_ref,
              wd_ref, bd_ref, y_hbm, xq, yacc, sem_x, sem_y):
    w = pl.program_id(0)
    f = pl.program_id(1)
    n_items = pl.num_programs(0)
    nt = nt_ref[w]
    slot = w % 2

    def x_copy(item, r, buf):
        return pltpu.make_async_copy(
            xs_hbm.at[pl.ds(pl.multiple_of(row0_ref[item] + r * MOE_TM, MOE_TM), MOE_TM), :],
            xq.at[buf, pl.ds(r * MOE_TM, MOE_TM), :], sem_x.at[buf])

    def y_copy(r):
        return pltpu.make_async_copy(
            yacc.at[pl.ds(pl.multiple_of(r * MOE_TM, MOE_TM), MOE_TM), :],
            y_hbm.at[pl.ds(pl.multiple_of(row0_ref[w] + r * MOE_TM, MOE_TM), MOE_TM), :], sem_y)

    def for_tiles(item, fn):
        for r in range(MOE_ITEM_TILES):
            @pl.when(r < nt_ref[item])
            def _():
                fn(r)

    @pl.when(f == 0)
    def _():
        @pl.when(w == 0)
        def _():
            for_tiles(0, lambda r: x_copy(0, r, 0).start())

        for_tiles(w, lambda r: x_copy(w, r, slot).wait())

        @pl.when(w + 1 < n_items)
        def _():
            nxt = jnp.minimum(w + 1, n_items - 1)
            for_tiles(nxt, lambda r: x_copy(nxt, r, 1 - slot).start())

    @pl.when(nt > 0)
    def _():
        def tile_rows(r):
            return pl.ds(pl.multiple_of(r * MOE_TM, MOE_TM), MOE_TM)

        def hidden(r):
            x = _unpack_bf16_pairs(xq[slot, tile_rows(r), :])
            gate = jnp.dot(x, wg_ref[...].astype(BF16), preferred_element_type=F32) + bg_ref[...]
            up = jnp.dot(x, wu_ref[...].astype(BF16), preferred_element_type=F32) + bu_ref[...]
            gate = jnp.minimum(gate, SWIGLU_LIMIT)
            up = jnp.clip(up, -SWIGLU_LIMIT, SWIGLU_LIMIT)
            return (gate * jax.nn.sigmoid(SWIGLU_ALPHA * gate) * (up + 1.0)).astype(BF16)

        def project(r, act):
            yacc[tile_rows(r), :] += jnp.dot(act, wd_ref[...].astype(BF16),
                                             preferred_element_type=F32)

            @pl.when(f == MOE_NF - 1)
            def _():
                y_copy(r).start()

        def tile(r, act_prev):
            act = hidden(r)
            project(r - 1, act_prev)
            return act

        @pl.when(f == 0)
        def _():
            def seed(r, carry):
                yacc[tile_rows(r), :] = jnp.broadcast_to(bd_ref[...], (MOE_TM, D_MODEL))
                return carry

            lax.fori_loop(0, nt, seed, 0)

        project(nt - 1, lax.fori_loop(1, nt, tile, hidden(0)))

        @pl.when(f == MOE_NF - 1)
        def _():
            for_tiles(w, lambda r: y_copy(r).wait())

    @pl.when((w == n_items - 1) & (f == MOE_NF - 1))
    def _():
        yacc[0:MOE_TM, :] = jnp.zeros((MOE_TM, D_MODEL), F32)

        def zero_tile(tix, carry):
            cp = pltpu.make_async_copy(
                yacc.at[pl.ds(0, MOE_TM), :],
                y_hbm.at[pl.ds(pl.multiple_of(tix * MOE_TM, MOE_TM), MOE_TM), :], sem_y)
            cp.start()
            cp.wait()
            return carry

        lax.fori_loop(nused_ref[0], y_hbm.shape[0] // MOE_TM, zero_tile, 0)


def _moe_ffn(item_e, item_row0, item_nt, n_used, xs, w_gate_up, b_gate_up, w_down, b_down, n_items):
    n_rows = xs.shape[0]

    def fidx(w, f, nt):
        return jnp.where(nt[w] > 0, f, MOE_NF - 1)

    grid_spec = pltpu.PrefetchScalarGridSpec(
        num_scalar_prefetch=4,
        grid=(n_items, MOE_NF),
        in_specs=[
            pl.BlockSpec(memory_space=pl.ANY),
            pl.BlockSpec((None, D_MODEL, MOE_TF),
                         lambda w, f, e, r0, nt, nu: (e[w], 0, fidx(w, f, nt))),
            pl.BlockSpec((None, D_MODEL, MOE_TF),
                         lambda w, f, e, r0, nt, nu: (e[w], 0, MOE_NF + fidx(w, f, nt))),
            pl.BlockSpec((None, 1, MOE_TF), lambda w, f, e, r0, nt, nu: (e[w], 0, fidx(w, f, nt))),
            pl.BlockSpec((None, 1, MOE_TF),
                         lambda w, f, e, r0, nt, nu: (e[w], 0, MOE_NF + fidx(w, f, nt))),
            pl.BlockSpec((None, MOE_TF, D_MODEL),
                         lambda w, f, e, r0, nt, nu: (e[w], fidx(w, f, nt), 0)),
            pl.BlockSpec((None, 1, D_MODEL), lambda w, f, e, r0, nt, nu: (e[w], 0, 0)),
        ],
        out_specs=pl.BlockSpec(memory_space=pl.ANY),
        scratch_shapes=[
            pltpu.VMEM((2, MOE_ITEM_ROWS, HALF), U32),
            pltpu.VMEM((MOE_ITEM_ROWS, D_MODEL), F32),
            pltpu.SemaphoreType.DMA((2,)),
            pltpu.SemaphoreType.DMA(()),
        ],
    )
    return pl.pallas_call(
        _moe_body,
        grid_spec=grid_spec,
        out_shape=jax.ShapeDtypeStruct((n_rows, D_MODEL), F32),
        compiler_params=_params(("arbitrary", "arbitrary")),
        name="moe_ffn",
    )(item_e, item_row0, item_nt, n_used, xs, w_gate_up, w_gate_up,
      b_gate_up.reshape(N_EXPERTS, 1, 2 * D_FF), b_gate_up.reshape(N_EXPERTS, 1, 2 * D_FF),
      w_down, b_down.reshape(N_EXPERTS, 1, D_MODEL))


COMB_TM = 128


def _combine_body(dest_ref, dnext_ref, y_hbm, gate_ref, h_ref, g_ref, o_ref, stage, sem):
    t = pl.program_id(0)
    n_steps = pl.num_programs(0)
    slot = t % 2

    def pull(idx_ref, r, k, buf):
        return pltpu.make_async_copy(
            y_hbm.at[pl.ds(idx_ref[0, r * TOP_K + k], 1), :],
            stage.at[buf, pl.ds(k * COMB_TM + r, 1), :], sem.at[buf])

    def issue_all(idx_ref, buf):
        def issue(r, carry):
            for k in range(TOP_K):
                pull(idx_ref, r, k, buf).start(priority=k % 2)
            return carry

        lax.fori_loop(0, COMB_TM, issue, 0)

    @pl.when(t == 0)
    def _():
        issue_all(dest_ref, 0)

    @pl.when(t + 1 < n_steps)
    def _():
        issue_all(dnext_ref, 1 - slot)

    pltpu.make_async_copy(y_hbm.at[pl.ds(0, TOP_K * COMB_TM), :], stage.at[slot],
                          sem.at[slot]).wait()

    gates = gate_ref[...]
    ff = gates[:, 0:1] * stage[slot, 0:COMB_TM, :]
    for k in range(1, TOP_K):
        ff = ff + gates[:, k:k + 1] * stage[slot, k * COMB_TM:(k + 1) * COMB_TM, :]
    h = h_ref[...] + ff
    inv = lax.rsqrt(jnp.mean(h * h, axis=-1, keepdims=True) + NORM_EPS)
    o_ref[...] = h * inv * g_ref[...]


def _combine(dest4, y_rows, gates, h1, g):
    t_real, d = h1.shape
    n_steps = t_real // COMB_TM
    dest_blocks = dest4.reshape(n_steps, 1, COMB_TM * TOP_K)
    smem_block = (None, 1, COMB_TM * TOP_K)
    return pl.pallas_call(
        _combine_body,
        grid=(n_steps,),
        in_specs=[
            pl.BlockSpec(smem_block, lambda i: (i, 0, 0), memory_space=pltpu.SMEM),
            pl.BlockSpec(smem_block, lambda i: (jnp.minimum(i + 1, n_steps - 1), 0, 0),
                         memory_space=pltpu.SMEM),
            pl.BlockSpec(memory_space=pl.ANY),
            pl.BlockSpec((COMB_TM, LANES), lambda i: (i, 0)),
            pl.BlockSpec((COMB_TM, d), lambda i: (i, 0)),
            pl.BlockSpec((1, d), lambda i: (0, 0)),
        ],
        out_specs=pl.BlockSpec((COMB_TM, d), lambda i: (i, 0)),
        out_shape=jax.ShapeDtypeStruct((t_real, d), F32),
        scratch_shapes=[
            pltpu.VMEM((2, TOP_K * COMB_TM, d), F32),
            pltpu.SemaphoreType.DMA((2,)),
        ],
        compiler_params=_params(("arbitrary",)),
        name="moe_combine",
    )(dest_blocks, dest_blocks, y_rows, gates, h1, g)


def kernel(x, meta_tokens, norm_mix_g, w_in, b_fgate, b_glu, conv_w, conv_b, gn_g, gn_b, w_out,
           norm_ffn_g, router_w, router_b, w_gate_up, b_gate_up, w_down, b_down, norm_final_g):
    assert x.shape[0] == 1 and norm_mix_g.shape[0] == 1
    xt = x[0]
    t_real = xt.shape[0]
    fcol = 3 * ATT_WIDTH
    w_all = w_in[0].astype(BF16)
    w_glu = w_all[:, fcol + N_HEADS:]
    w_f = jnp.pad(w_all[:, fcol:fcol + N_HEADS], ((0, 0), (0, LANES - N_HEADS)))
    g_mix = norm_mix_g[0].reshape(1, D_MODEL)

    proj, f_real = _inproj(xt, g_mix, w_all, w_glu, w_f, tm=1024, tn=1024)
    proj_meta, f_meta = _inproj(meta_tokens, g_mix, w_all, w_glu, w_f, tm=N_META, tn=512)

    b_f = jnp.pad(b_fgate[0], (0, LANES - N_HEADS)).reshape(1, LANES)
    qa, ka, kam = _fgate_bias(f_meta, f_real, b_f)
    v_meta = proj_meta[:, 2 * ATT_WIDTH:3 * ATT_WIDTH]
    vm_t = v_meta.reshape(N_META, N_HEADS, HEAD_DIM).transpose(1, 2, 0)
    att = _attention(proj, proj_meta, vm_t, qa, ka, kam, tq=512, tk=256)
    conv = _conv_mixer(proj, proj_meta, b_glu[0].reshape(1, -1), conv_w[0],
                       conv_b[0].reshape(1, -1), gn_g[0].reshape(1, -1), gn_b[0].reshape(1, -1),
                       tm=512)

    rw = jnp.pad(router_w[0], ((0, 0), (0, LANES - N_EXPERTS)))
    rw_hi = rw.astype(BF16)
    rw_lo = (rw - rw_hi.astype(F32)).astype(BF16)
    rb = jnp.pad(router_b[0], (0, LANES - N_EXPERTS), constant_values=NEG_BIG).reshape(1, LANES)
    h1, hn_packed, top_i, gates = _outproj_router(
        att, conv, xt, w_out[0].astype(BF16), norm_ffn_g[0].reshape(1, D_MODEL),
        rw_hi, rw_lo, rb, tm=256)

    n_assign = t_real * TOP_K
    n_tiles = (n_assign + N_EXPERTS * (MOE_TM - 1)) // MOE_TM + 1
    n_items = N_EXPERTS + (n_tiles * MOE_TM) // MOE_ITEM_ROWS
    dest, plan = _routing_plan(top_i)
    counts, starts, tiles_e = (plan[r, :N_EXPERTS] for r in range(3))
    n_used = jnp.sum(tiles_e).astype(jnp.int32).reshape(1)
    item_e, item_row0, item_nt = _item_tables(counts, starts, tiles_e, n_items)
    dest4 = dest[:, :TOP_K]

    xs = _dispatch(starts, tiles_e, n_used, dest4, hn_packed, n_tiles)
    y_rows = _moe_ffn(item_e, item_row0, item_nt, n_used, xs, w_gate_up[0], b_gate_up[0],
                      w_down[0], b_down[0], n_items)
    out = _combine(dest4, y_rows, gates, h1, norm_final_g.reshape(1, D_MODEL))
    return out[None]
```

```python
import functools

import jax
import jax.numpy as jnp
import numpy as np
from jax import lax
from jax.experimental import pallas as pl
from jax.experimental.pallas import tpu as pltpu

D_MODEL = 2048
N_META = 16
HEAD_DIM = 128
ATT_WIDTH = 1024
N_HEADS = ATT_WIDTH // HEAD_DIM
CONV_CH = 1024
N_GROUPS = 8
GROUP_CH = CONV_CH // N_GROUPS
CONV_K = 31
N_EXPERTS = 32
TOP_K = 4
D_FF = 2048
SWIGLU_LIMIT = 7.0
SWIGLU_ALPHA = 1.702
NORM_EPS = 1e-5
NEG_BIG = -1e30

LANES = 128
SUBLANES = 8
V7X_VMEM_BYTES = 64 * 1024 * 1024
VMEM_LIMIT = V7X_VMEM_BYTES * 7 // 8

F32 = jnp.float32
BF16 = jnp.bfloat16


def _params(sem, vmem=VMEM_LIMIT):
    return pltpu.CompilerParams(dimension_semantics=sem, vmem_limit_bytes=vmem)


def _inproj_body(x_ref, g_ref, wa_ref, wc_ref, wf_ref, o_ref, f_ref, hn_ref, *, n_att):
    j = pl.program_id(1)

    @pl.when(j == 0)
    def _():
        x = x_ref[...]
        inv = lax.rsqrt(jnp.mean(x * x, axis=-1, keepdims=True) + NORM_EPS)
        hn = (x * inv * g_ref[...]).astype(BF16)
        hn_ref[...] = hn
        f_ref[...] = jnp.dot(hn, wf_ref[...], preferred_element_type=F32)

    @pl.when(j < n_att)
    def _():
        o_ref[...] = jnp.dot(hn_ref[...], wa_ref[...],
                             preferred_element_type=F32).astype(o_ref.dtype)

    @pl.when(j >= n_att)
    def _():
        o_ref[...] = jnp.dot(hn_ref[...], wc_ref[...],
                             preferred_element_type=F32).astype(o_ref.dtype)


def _inproj(x, g, w_all, w_glu, w_f, tm, tn):
    m, d = x.shape
    n_att = 3 * ATT_WIDTH // tn
    n = 3 * ATT_WIDTH + w_glu.shape[1]
    return pl.pallas_call(
        functools.partial(_inproj_body, n_att=n_att),
        grid=(m // tm, n // tn),
        in_specs=[
            pl.BlockSpec((tm, d), lambda i, j: (i, 0)),
            pl.BlockSpec((1, d), lambda i, j: (0, 0)),
            pl.BlockSpec((d, tn), lambda i, j: (0, jnp.minimum(j, n_att - 1))),
            pl.BlockSpec((d, tn), lambda i, j: (0, jnp.maximum(j - n_att, 0))),
            pl.BlockSpec((d, LANES), lambda i, j: (0, 0)),
        ],
        out_specs=[
            pl.BlockSpec((tm, tn), lambda i, j: (i, j)),
            pl.BlockSpec((tm, LANES), lambda i, j: (i, 0)),
        ],
        out_shape=[
            jax.ShapeDtypeStruct((m, n), BF16),
            jax.ShapeDtypeStruct((m, LANES), F32),
        ],
        scratch_shapes=[pltpu.VMEM((tm, d), BF16)],
        compiler_params=_params(("arbitrary", "arbitrary")),
        name="inproj",
    )(x, g, w_all, w_glu, w_f)


ATT_SCALE = 1.0 / float(np.sqrt(HEAD_DIM))
EXP2_MULT = ATT_SCALE * float(np.log2(np.e))
N_SPLIT = 3
FGATE_TB = 512


def _log_sigmoid(x):
    return jnp.minimum(x, 0.0) - jnp.log1p(jnp.exp(-jnp.abs(x)))


def _prefix_rows(lf):
    n = lf.shape[0]
    row = lax.broadcasted_iota(jnp.int32, (n, n), 0)
    col = lax.broadcasted_iota(jnp.int32, (n, n), 1)
    tri = (col <= row).astype(F32)
    return jnp.dot(tri, lf, preferred_element_type=F32, precision=lax.Precision.HIGHEST)


def _bias_columns(c, head):
    lane = lax.broadcasted_iota(jnp.int32, c.shape, 1)
    x = jnp.sum(jnp.where(lane == head, c, 0.0), axis=-1, keepdims=True) * float(np.sqrt(HEAD_DIM))
    hi = x.astype(BF16).astype(F32)
    r1 = x - hi
    mid = r1.astype(BF16).astype(F32)
    lo = r1 - mid
    part = lane % N_SPLIT
    parts = jnp.where(part == 0, hi, jnp.where(part == 1, mid, lo))
    qa = jnp.where(lane < N_SPLIT, parts, jnp.where(lane < 2 * N_SPLIT, 1.0, 0.0))
    ka = jnp.where(lane < N_SPLIT, 1.0, jnp.where(lane < 2 * N_SPLIT, -parts, 0.0))
    return qa.astype(BF16), ka.astype(BF16)


def _fgate_body(fm_ref, f_ref, b_ref, qa_ref, ka_ref, kam_ref, carry_ref):
    @pl.when(pl.program_id(0) == 0)
    def _():
        cm = _prefix_rows(_log_sigmoid(fm_ref[...] + b_ref[...]))
        carry_ref[...] = cm[N_META - 1:N_META, :]
        for head in range(N_HEADS):
            kam_ref[head] = _bias_columns(cm, head)[1]

    c = _prefix_rows(_log_sigmoid(f_ref[...] + b_ref[...])) + carry_ref[...]
    carry_ref[...] = c[FGATE_TB - 1:FGATE_TB, :]
    for head in range(N_HEADS):
        qa, ka = _bias_columns(c, head)
        qa_ref[head] = qa
        ka_ref[head] = ka


def _fgate_bias(f_meta, f_real, b_f):
    t_real = f_real.shape[0]
    return pl.pallas_call(
        _fgate_body,
        grid=(t_real // FGATE_TB,),
        in_specs=[
            pl.BlockSpec((N_META, LANES), lambda i: (0, 0)),
            pl.BlockSpec((FGATE_TB, LANES), lambda i: (i, 0)),
            pl.BlockSpec((1, LANES), lambda i: (0, 0)),
        ],
        out_specs=[
            pl.BlockSpec((N_HEADS, FGATE_TB, LANES), lambda i: (0, i, 0)),
            pl.BlockSpec((N_HEADS, FGATE_TB, LANES), lambda i: (0, i, 0)),
            pl.BlockSpec((N_HEADS, N_META, LANES), lambda i: (0, 0, 0)),
        ],
        out_shape=[
            jax.ShapeDtypeStruct((N_HEADS, t_real, LANES), BF16),
            jax.ShapeDtypeStruct((N_HEADS, t_real, LANES), BF16),
            jax.ShapeDtypeStruct((N_HEADS, N_META, LANES), BF16),
        ],
        scratch_shapes=[pltpu.VMEM((1, LANES), F32)],
        compiler_params=_params(("arbitrary",)),
        name="fgate_bias",
    )(f_meta, f_real, b_f)


ATT_PACK_ROWS = 16
ATT_HEADS_PER_STEP = 2


def _attn_body(q_ref, qa_ref, k_ref, ka_ref, v_ref, km_ref, kam_ref, vmt_ref, o_ref,
               vt_ref, s_a, s_b, p_ref, acc_ref, *, tq):
    i = pl.program_id(1)
    heads, n_blocks, _, tk = vt_ref.shape
    assert tq == 2 * tk
    contract_last = (((1,), (1,)), ((), ()))

    def cols(hh):
        return slice(hh * HEAD_DIM, (hh + 1) * HEAD_DIM)

    @pl.when(i == 0)
    def _():
        for hh in range(heads):
            for blk in range(n_blocks):
                vt_ref[hh, blk] = v_ref[blk * tk:(blk + 1) * tk, cols(hh)].T

    q = [jnp.concatenate([q_ref[:, cols(hh)], qa_ref[hh]], axis=1) for hh in range(heads)]

    def scores_t(hh, k, ka):
        return lax.dot_general(jnp.concatenate([k, ka], axis=1), q[hh], contract_last,
                               preferred_element_type=F32)

    def block_scores(hh, j):
        rows = pl.ds(pl.multiple_of(j * tk, tk), tk)
        return scores_t(hh, k_ref[rows, cols(hh)], ka_ref[hh, rows, :])

    stats = []
    for hh in range(heads):
        s = scores_t(hh, km_ref[:, cols(hh)], kam_ref[hh])
        m = jnp.max(s, axis=0, keepdims=True)
        p = jnp.exp2((s - m) * EXP2_MULT)
        stats += [m, jnp.sum(p, axis=0, keepdims=True)]
        acc_ref[hh] = jnp.dot(vmt_ref[hh], p.astype(BF16), preferred_element_type=F32)

    def absorb(hh, s_ref, j, m_prev, l_prev):
        top = s_ref[hh, 0:SUBLANES, :]
        for r in range(SUBLANES, tk, SUBLANES):
            top = jnp.maximum(top, s_ref[hh, r:r + SUBLANES, :])
        m_new = jnp.maximum(m_prev, jnp.max(top, axis=0, keepdims=True))
        alpha = jnp.exp2((m_prev - m_new) * EXP2_MULT)
        part = jnp.zeros((SUBLANES, tq), F32)
        for r in range(0, tk, ATT_PACK_ROWS):
            p = jnp.exp2((s_ref[hh, r:r + ATT_PACK_ROWS, :] - m_new) * EXP2_MULT)
            part = part + p[0:SUBLANES, :] + p[SUBLANES:, :]
            p_ref[hh, r:r + ATT_PACK_ROWS, :] = p.astype(BF16)
        l_new = alpha * l_prev + jnp.sum(part, axis=0, keepdims=True)
        acc_ref[hh] = alpha * acc_ref[hh] + jnp.dot(vt_ref[hh, j], p_ref[hh],
                                                    preferred_element_type=F32)
        return [m_new, l_new]

    def absorb_all(s_ref, j, st):
        out = []
        for hh in range(heads):
            out += absorb(hh, s_ref, j, st[2 * hh], st[2 * hh + 1])
        return out

    def pair(pi, st):
        for hh in range(heads):
            s_b[hh] = block_scores(hh, 2 * pi + 1)
        st = absorb_all(s_a, 2 * pi, st)
        for hh in range(heads):
            s_a[hh] = block_scores(hh, 2 * pi + 2)
        return tuple(absorb_all(s_b, 2 * pi + 1, st))

    for hh in range(heads):
        s_a[hh] = block_scores(hh, 0)
    stats = lax.fori_loop(0, i, pair, tuple(stats))

    key = lax.broadcasted_iota(jnp.int32, (tk, tq), 0)
    qry = lax.broadcasted_iota(jnp.int32, (tk, tq), 1)
    for hh in range(heads):
        s_b[hh] = jnp.where(key + tk <= qry, block_scores(hh, 2 * i + 1), NEG_BIG)
        s_a[hh] = jnp.where(key <= qry, s_a[hh], NEG_BIG)
    stats = absorb_all(s_b, 2 * i + 1, absorb_all(s_a, 2 * i, stats))
    for hh in range(heads):
        o_ref[:, cols(hh)] = (acc_ref[hh] / stats[2 * hh + 1]).T.astype(o_ref.dtype)


def _attention(proj, proj_meta, vm_t, qa, ka, kam, tq, tk):
    t_real = proj.shape[0]
    hs = ATT_HEADS_PER_STEP
    width = hs * HEAD_DIM
    kcol = ATT_WIDTH // width
    vcol = 2 * ATT_WIDTH // width
    return pl.pallas_call(
        functools.partial(_attn_body, tq=tq),
        grid=(N_HEADS // hs, t_real // tq),
        in_specs=[
            pl.BlockSpec((tq, width), lambda g, i: (i, g)),
            pl.BlockSpec((hs, tq, LANES), lambda g, i: (g, i, 0)),
            pl.BlockSpec((t_real, width), lambda g, i: (0, kcol + g)),
            pl.BlockSpec((hs, t_real, LANES), lambda g, i: (g, 0, 0)),
            pl.BlockSpec((t_real, width), lambda g, i: (0, vcol + g)),
            pl.BlockSpec((N_META, width), lambda g, i: (0, kcol + g)),
            pl.BlockSpec((hs, N_META, LANES), lambda g, i: (g, 0, 0)),
            pl.BlockSpec((hs, HEAD_DIM, N_META), lambda g, i: (g, 0, 0)),
        ],
        out_specs=pl.BlockSpec((tq, width), lambda g, i: (i, g)),
        out_shape=jax.ShapeDtypeStruct((t_real, ATT_WIDTH), BF16),
        scratch_shapes=[
            pltpu.VMEM((hs, t_real // tk, HEAD_DIM, tk), BF16),
            pltpu.VMEM((hs, tk, tq), F32),
            pltpu.VMEM((hs, tk, tq), F32),
            pltpu.VMEM((hs, tk, tq), BF16),
            pltpu.VMEM((hs, HEAD_DIM, tq), F32),
        ],
        compiler_params=_params(("arbitrary", "arbitrary")),
        name="fox_attention",
    )(proj, qa, proj, ka, proj, proj_meta, kam, vm_t)


CONV_HIST = 32
CONV_ROWS = 128


def _glu(a_ref, g_ref, bglu_ref):
    a = a_ref[...].astype(F32) + bglu_ref[:, :CONV_CH]
    g = g_ref[...].astype(F32) + bglu_ref[:, CONV_CH:]
    return a * jax.nn.sigmoid(g)


def _conv_body(a_ref, g_ref, am_ref, gm_ref, bglu_ref, w_ref, cb_ref, gng_ref, gnb_ref,
               o_ref, ubuf, *, tm):
    i = pl.program_id(0)

    @pl.when(i == 0)
    def _():
        ubuf[0:CONV_HIST - N_META, :] = jnp.zeros((CONV_HIST - N_META, CONV_CH), F32)
        ubuf[CONV_HIST - N_META:CONV_HIST, :] = _glu(am_ref, gm_ref, bglu_ref)

    @pl.when(i > 0)
    def _():
        ubuf[0:CONV_HIST, :] = ubuf[tm:tm + CONV_HIST, :]

    ubuf[CONV_HIST:CONV_HIST + tm, :] = _glu(a_ref, g_ref, bglu_ref)

    first = CONV_HIST - (CONV_K - 1)
    for grp in range(N_GROUPS):
        lanes = slice(grp * GROUP_CH, (grp + 1) * GROUP_CH)
        for rc in range(tm // CONV_ROWS):
            base = rc * CONV_ROWS
            acc = jnp.zeros((CONV_ROWS, GROUP_CH), F32)
            for phase in range(SUBLANES):
                slab = CONV_ROWS + (SUBLANES if phase else 0)
                part = jnp.zeros((slab, GROUP_CH), F32)
                for tap in range(CONV_K):
                    if (first + tap) % SUBLANES == phase:
                        row0 = base + (first + tap) // SUBLANES * SUBLANES
                        part = part + ubuf[row0:row0 + slab, lanes] * w_ref[tap:tap + 1, lanes]
                acc = acc + part[phase:phase + CONV_ROWS, :]
            acc = acc + cb_ref[:, lanes]
            mu = jnp.mean(acc, axis=-1, keepdims=True)
            dlt = acc - mu
            var = jnp.mean(dlt * dlt, axis=-1, keepdims=True)
            y = dlt * lax.rsqrt(var + NORM_EPS) * gng_ref[:, lanes] + gnb_ref[:, lanes]
            o_ref[base:base + CONV_ROWS, lanes] = (y * jax.nn.sigmoid(y)).astype(o_ref.dtype)


def _conv_mixer(proj, proj_meta, b_glu, conv_w, conv_b, gn_g, gn_b, tm):
    t_real = proj.shape[0]
    acol = 3 * ATT_WIDTH // CONV_CH
    gcol = acol + 1
    const = lambda i: (0, 0)
    return pl.pallas_call(
        functools.partial(_conv_body, tm=tm),
        grid=(t_real // tm,),
        in_specs=[
            pl.BlockSpec((tm, CONV_CH), lambda i: (i, acol)),
            pl.BlockSpec((tm, CONV_CH), lambda i: (i, gcol)),
            pl.BlockSpec((N_META, CONV_CH), lambda i: (0, acol)),
            pl.BlockSpec((N_META, CONV_CH), lambda i: (0, gcol)),
            pl.BlockSpec((1, 2 * CONV_CH), const),
            pl.BlockSpec((CONV_K, CONV_CH), const),
            pl.BlockSpec((1, CONV_CH), const),
            pl.BlockSpec((1, CONV_CH), const),
            pl.BlockSpec((1, CONV_CH), const),
        ],
        out_specs=pl.BlockSpec((tm, CONV_CH), lambda i: (i, 0)),
        out_shape=jax.ShapeDtypeStruct((t_real, CONV_CH), BF16),
        scratch_shapes=[pltpu.VMEM((tm + CONV_HIST, CONV_CH), F32)],
        compiler_params=_params(("arbitrary",)),
        name="conv_mixer",
    )(proj, proj, proj_meta, proj_meta, b_glu, conv_w, conv_b, gn_g, gn_b)


def _outproj_body(att_ref, conv_ref, x_ref, w_ref, g_ref, rwh_ref, rwl_ref, rb_ref,
                  h_ref, hn_ref, topi_ref, gate_ref):
    mix = jnp.dot(att_ref[...], w_ref[0:ATT_WIDTH, :], preferred_element_type=F32)
    mix = mix + jnp.dot(conv_ref[...], w_ref[ATT_WIDTH:, :], preferred_element_type=F32)
    h = x_ref[...] + mix
    h_ref[...] = h
    inv = lax.rsqrt(jnp.mean(h * h, axis=-1, keepdims=True) + NORM_EPS)
    hn = h * inv * g_ref[...]
    hn_ref[...] = hn
    hn_hi = hn.astype(BF16)

    hn_lo = (hn - hn_hi.astype(F32)).astype(BF16)
    logits = jnp.dot(hn_hi, rwh_ref[...], preferred_element_type=F32)
    logits = logits + jnp.dot(hn_hi, rwl_ref[...], preferred_element_type=F32)
    logits = logits + jnp.dot(hn_lo, rwh_ref[...], preferred_element_type=F32) + rb_ref[...]
    lane = lax.broadcasted_iota(jnp.int32, logits.shape, 1)
    vals = logits
    tops = []
    idxs = []
    for _ in range(TOP_K):
        mx = jnp.max(vals, axis=-1, keepdims=True)
        ix = jnp.min(jnp.where(vals == mx, lane, LANES), axis=-1, keepdims=True)
        tops.append(mx)
        idxs.append(ix)
        vals = jnp.where(lane == ix, -jnp.inf, vals)
    exps = [jnp.exp(v - tops[0]) for v in tops]
    denom = exps[0] + exps[1] + exps[2] + exps[3]
    topi = jnp.full(logits.shape, -1, jnp.int32)
    gate = jnp.zeros(logits.shape, F32)
    for k in range(TOP_K):
        topi = jnp.where(lane == k, idxs[k], topi)
        gate = jnp.where(lane == k, exps[k] / denom, gate)
    topi_ref[...] = topi
    gate_ref[...] = gate


def _outproj_router(att, conv, x, w_out, g, rw_hi, rw_lo, router_b, tm):
    t_real, d = x.shape
    const = lambda i: (0, 0)
    return pl.pallas_call(
        _outproj_body,
        grid=(t_real // tm,),
        in_specs=[
            pl.BlockSpec((tm, ATT_WIDTH), lambda i: (i, 0)),
            pl.BlockSpec((tm, CONV_CH), lambda i: (i, 0)),
            pl.BlockSpec((tm, d), lambda i: (i, 0)),
            pl.BlockSpec((d, d), const),
            pl.BlockSpec((1, d), const),
            pl.BlockSpec((d, LANES), const),
            pl.BlockSpec((d, LANES), const),
            pl.BlockSpec((1, LANES), const),
        ],
        out_specs=[
            pl.BlockSpec((tm, d), lambda i: (i, 0)),
            pl.BlockSpec((tm, d), lambda i: (i, 0)),
            pl.BlockSpec((tm, LANES), lambda i: (i, 0)),
            pl.BlockSpec((tm, LANES), lambda i: (i, 0)),
        ],
        out_shape=[
            jax.ShapeDtypeStruct((t_real, d), F32),
            jax.ShapeDtypeStruct((t_real, d), F32),
            jax.ShapeDtypeStruct((t_real, LANES), jnp.int32),
            jax.ShapeDtypeStruct((t_real, LANES), F32),
        ],
        compiler_params=_params(("arbitrary",)),
        name="outproj_router",
    )(att, conv, x, w_out, g, rw_hi, rw_lo, router_b)


MOE_TM = 256
PLAN_TB = 512
PLAN_ROWS = SUBLANES


def _plan_body(topi_ref, dest_ref, meta_ref, cnt_ref, carry_ref, start_ref):
    phase = pl.program_id(0)
    j = pl.program_id(1)
    topi = topi_ref[...]
    lane = lax.broadcasted_iota(jnp.int32, topi.shape, 1)
    picks = [topi[:, k:k + 1] for k in range(TOP_K)]
    onehot = jnp.zeros(topi.shape, F32)
    for pick in picks:
        onehot = onehot + (lane == pick).astype(F32)
    colsum = jnp.sum(onehot, axis=0, keepdims=True)

    @pl.when((phase == 0) & (j == 0))
    def _():
        cnt_ref[...] = jnp.zeros(cnt_ref.shape, F32)

    @pl.when(phase == 0)
    def _():
        cnt_ref[...] += colsum

    @pl.when((phase == 1) & (j == 0))
    def _():
        cnt = jnp.broadcast_to(cnt_ref[...], (SUBLANES, LANES))
        tiles = jnp.floor((cnt + (MOE_TM - 1)) * (1.0 / MOE_TM))
        padded = tiles * MOE_TM
        row = lax.broadcasted_iota(jnp.int32, (LANES, LANES), 0)
        col = lax.broadcasted_iota(jnp.int32, (LANES, LANES), 1)
        upper = (row <= col).astype(F32)
        pad_end = jnp.dot(padded, upper, preferred_element_type=F32,
                          precision=lax.Precision.HIGHEST)
        start = pad_end - padded
        start_ref[...] = start[0:1, :]
        carry_ref[...] = jnp.zeros(carry_ref.shape, F32)
        sub = lax.broadcasted_iota(jnp.int32, (PLAN_ROWS, LANES), 0)
        meta_ref[...] = jnp.where(sub == 0, cnt, jnp.where(sub == 1, start, tiles)).astype(jnp.int32)

    @pl.when(phase == 1)
    def _():
        row = lax.broadcasted_iota(jnp.int32, (PLAN_TB, PLAN_TB), 0)
        col = lax.broadcasted_iota(jnp.int32, (PLAN_TB, PLAN_TB), 1)
        earlier = (col < row).astype(BF16)
        before = jnp.dot(earlier, onehot.astype(BF16), preferred_element_type=F32)
        slot = before + carry_ref[...] + start_ref[...]
        dest = jnp.zeros(topi.shape, jnp.int32)
        for k, pick in enumerate(picks):
            d_k = jnp.sum(jnp.where(lane == pick, slot, 0.0), axis=-1, keepdims=True)
            dest = jnp.where(lane == k, d_k.astype(jnp.int32), dest)
        dest_ref[...] = dest
        carry_ref[...] += colsum


def _routing_plan(top_i):
    t_real = top_i.shape[0]
    return pl.pallas_call(
        _plan_body,
        grid=(2, t_real // PLAN_TB),
        in_specs=[pl.BlockSpec((PLAN_TB, LANES), lambda p, j: (j, 0))],
        out_specs=[
            pl.BlockSpec((PLAN_TB, LANES), lambda p, j: (j * p, 0)),
            pl.BlockSpec((PLAN_ROWS, LANES), lambda p, j: (0, 0)),
        ],
        out_shape=[
            jax.ShapeDtypeStruct((t_real, LANES), jnp.int32),
            jax.ShapeDtypeStruct((PLAN_ROWS, LANES), jnp.int32),
        ],
        scratch_shapes=[pltpu.VMEM((1, LANES), F32)] * 3,
        compiler_params=_params(("arbitrary", "arbitrary")),
        name="routing_plan",
    )(top_i)


MOE_ITEM_TILES = 5
MOE_ITEM_ROWS = MOE_ITEM_TILES * MOE_TM


def _item_tables(counts, starts, tiles_e, n_items):
    items_e = (tiles_e + MOE_ITEM_TILES - 1) // MOE_ITEM_TILES
    item_end = jnp.cumsum(items_e)
    item_start = item_end - items_e
    wid = jnp.arange(n_items, dtype=jnp.int32)
    e_of = jnp.minimum(jnp.sum((item_end[None, :] <= wid[:, None]).astype(jnp.int32), axis=1),
                       N_EXPERTS - 1)
    k_of = wid - item_start[e_of]
    live = wid < item_end[-1]
    last_e = jnp.max(jnp.where(counts > 0, jnp.arange(N_EXPERTS, dtype=jnp.int32), 0))
    item_e = jnp.where(live, e_of, last_e).astype(jnp.int32)
    item_nt = jnp.where(live, jnp.minimum(MOE_ITEM_TILES, tiles_e[e_of] - k_of * MOE_ITEM_TILES), 0)
    item_row0 = jnp.where(live, starts[e_of] + k_of * MOE_ITEM_ROWS, 0)
    return item_e, item_row0.astype(jnp.int32), item_nt.astype(jnp.int32)


DISP_TB = 256


def _dispatch_body(start_ref, tiles_ref, nused_ref, dest_ref, src_ref, xs_hbm,
                   rows, ztile, sem, zsem):
    t = pl.program_id(0)
    n_steps = pl.num_programs(0)
    n_tiles = xs_hbm.shape[0] // MOE_TM
    pushes = DISP_TB * TOP_K
    slot = t % 2
    rows[slot] = src_ref[...]

    @pl.when(t == 0)
    def _():
        ztile[...] = jnp.zeros(ztile.shape, ztile.dtype)

        def clear(tix):
            return pltpu.make_async_copy(
                ztile, xs_hbm.at[pl.ds(pl.multiple_of(tix * MOE_TM, MOE_TM), MOE_TM), :], zsem)

        def last_tile(e):
            return start_ref[e] // MOE_TM + tiles_ref[e] - 1

        def start_group(e, carry):
            @pl.when(tiles_ref[e] > 0)
            def _():
                clear(last_tile(e)).start()
            return carry

        def wait_group(e, carry):
            @pl.when(tiles_ref[e] > 0)
            def _():
                clear(last_tile(e)).wait()
            return carry

        lax.fori_loop(0, N_EXPERTS, start_group, 0)
        lax.fori_loop(nused_ref[0], n_tiles, lambda tix, c: (clear(tix).start(), c)[1], 0)
        lax.fori_loop(0, N_EXPERTS, wait_group, 0)
        lax.fori_loop(nused_ref[0], n_tiles, lambda tix, c: (clear(tix).wait(), c)[1], 0)

    def issue(r, carry):
        for k in range(TOP_K):
            pltpu.make_async_copy(
                rows.at[slot, pl.ds(r, 1), :],
                xs_hbm.at[pl.ds(dest_ref[0, r * TOP_K + k], 1), :],
                sem.at[slot]).start(priority=k % 2)
        return carry

    lax.fori_loop(0, DISP_TB, issue, 0)

    def retire_step(buf):
        pltpu.make_async_copy(xs_hbm.at[pl.ds(0, pushes), :], xs_hbm.at[pl.ds(0, pushes), :],
                              sem.at[buf]).wait()

    @pl.when(t > 0)
    def _():
        retire_step(1 - slot)

    @pl.when(t == n_steps - 1)
    def _():
        retire_step(slot)


def _dispatch(starts, tiles_e, n_used, dest4, hn_rows, n_tiles):
    t_real = hn_rows.shape[0]
    n_steps = t_real // DISP_TB
    grid_spec = pltpu.PrefetchScalarGridSpec(
        num_scalar_prefetch=3,
        grid=(n_steps,),
        in_specs=[
            pl.BlockSpec((None, 1, DISP_TB * TOP_K), lambda t, *_: (t, 0, 0),
                         memory_space=pltpu.SMEM),
            pl.BlockSpec((DISP_TB, D_MODEL), lambda t, *_: (t, 0)),
        ],
        out_specs=pl.BlockSpec(memory_space=pl.ANY),
        scratch_shapes=[
            pltpu.VMEM((2, DISP_TB, D_MODEL), F32),
            pltpu.VMEM((MOE_TM, D_MODEL), F32),
            pltpu.SemaphoreType.DMA((2,)),
            pltpu.SemaphoreType.DMA(()),
        ],
    )
    return pl.pallas_call(
        _dispatch_body,
        grid_spec=grid_spec,
        out_shape=jax.ShapeDtypeStruct((n_tiles * MOE_TM, D_MODEL), F32),
        compiler_params=_params(("arbitrary",)),
        name="moe_dispatch",
    )(starts, tiles_e, n_used, dest4.reshape(n_steps, 1, DISP_TB * TOP_K), hn_rows)


MOE_TF = 512
MOE_NF = D_FF // MOE_TF


def _moe_body(e_ref, row0_ref, nt_ref, nused_ref, xs_hbm, wg_ref, wu_ref, bg_ref, bu_ref,
              wd_ref, bd_ref, y_hbm, xq, yacc, sem_x, sem_y):
    w = pl.program_id(0)
    f = pl.program_id(1)
    n_items = pl.num_programs(0)
    nt = nt_ref[w]

    def x_copy(r):
        return pltpu.make_async_copy(
            xs_hbm.at[pl.ds(pl.multiple_of(row0_ref[w] + r * MOE_TM, MOE_TM), MOE_TM), :],
            xq.at[pl.ds(pl.multiple_of(r * MOE_TM, MOE_TM), MOE_TM), :], sem_x.at[r])

    def y_copy(r):
        return pltpu.make_async_copy(
            yacc.at[pl.ds(pl.multiple_of(r * MOE_TM, MOE_TM), MOE_TM), :],
            y_hbm.at[pl.ds(pl.multiple_of(row0_ref[w] + r * MOE_TM, MOE_TM), MOE_TM), :], sem_y)

    def for_tiles(item, fn):
        for r in range(MOE_ITEM_TILES):
            @pl.when(r < nt_ref[item])
            def _():
                fn(r)

    @pl.when(f == 0)
    def _():
        for_tiles(w, lambda r: x_copy(r).start())

    @pl.when(nt > 0)
    def _():
        def tile_rows(r):
            return pl.ds(pl.multiple_of(r * MOE_TM, MOE_TM), MOE_TM)

        def hidden(r):
            @pl.when(f == 0)
            def _():
                x_copy(r).wait()

            x = xq[tile_rows(r), :].astype(BF16)
            gate = jnp.dot(x, wg_ref[...].astype(BF16), preferred_element_type=F32) + bg_ref[...]
            up = jnp.dot(x, wu_ref[...].astype(BF16), preferred_element_type=F32) + bu_ref[...]
            gate = jnp.minimum(gate, SWIGLU_LIMIT)
            up = jnp.clip(up, -SWIGLU_LIMIT, SWIGLU_LIMIT)
            return (gate * jax.nn.sigmoid(SWIGLU_ALPHA * gate) * (up + 1.0)).astype(BF16)

        def project(r, act):
            yacc[tile_rows(r), :] += jnp.dot(act, wd_ref[...].astype(BF16),
                                             preferred_element_type=F32)

            @pl.when(f == MOE_NF - 1)
            def _():
                y_copy(r).start()

        def tile(r, act_prev):
            act = hidden(r)
            project(r - 1, act_prev)
            return act

        @pl.when(f == 0)
        def _():
            def seed(r, carry):
                yacc[tile_rows(r), :] = jnp.broadcast_to(bd_ref[...], (MOE_TM, D_MODEL))
                return carry

            lax.fori_loop(0, nt, seed, 0)

        project(nt - 1, lax.fori_loop(1, nt, tile, hidden(0)))

        @pl.when(f == MOE_NF - 1)
        def _():
            for_tiles(w, lambda r: y_copy(r).wait())

    @pl.when((w == n_items - 1) & (f == MOE_NF - 1))
    def _():
        yacc[0:MOE_TM, :] = jnp.zeros((MOE_TM, D_MODEL), F32)

        def zero_tile(tix, carry):
            cp = pltpu.make_async_copy(
                yacc.at[pl.ds(0, MOE_TM), :],
                y_hbm.at[pl.ds(pl.multiple_of(tix * MOE_TM, MOE_TM), MOE_TM), :], sem_y)
            cp.start()
            cp.wait()
            return carry

        lax.fori_loop(nused_ref[0], y_hbm.shape[0] // MOE_TM, zero_tile, 0)


def _moe_ffn(item_e, item_row0, item_nt, n_used, xs, w_gate_up, b_gate_up, w_down, b_down, n_items):
    n_rows = xs.shape[0]

    def fidx(w, f, nt):
        return jnp.where(nt[w] > 0, f, MOE_NF - 1)

    grid_spec = pltpu.PrefetchScalarGridSpec(
        num_scalar_prefetch=4,
        grid=(n_items, MOE_NF),
        in_specs=[
            pl.BlockSpec(memory_space=pl.ANY),
            pl.BlockSpec((None, D_MODEL, MOE_TF),
                         lambda w, f, e, r0, nt, nu: (e[w], 0, fidx(w, f, nt))),
            pl.BlockSpec((None, D_MODEL, MOE_TF),
                         lambda w, f, e, r0, nt, nu: (e[w], 0, MOE_NF + fidx(w, f, nt))),
            pl.BlockSpec((None, 1, MOE_TF), lambda w, f, e, r0, nt, nu: (e[w], 0, fidx(w, f, nt))),
            pl.BlockSpec((None, 1, MOE_TF),
                         lambda w, f, e, r0, nt, nu: (e[w], 0, MOE_NF + fidx(w, f, nt))),
            pl.BlockSpec((None, MOE_TF, D_MODEL),
                         lambda w, f, e, r0, nt, nu: (e[w], fidx(w, f, nt), 0)),
            pl.BlockSpec((None, 1, D_MODEL), lambda w, f, e, r0, nt, nu: (e[w], 0, 0)),
        ],
        out_specs=pl.BlockSpec(memory_space=pl.ANY),
        scratch_shapes=[
            pltpu.VMEM((MOE_ITEM_ROWS, D_MODEL), F32),
            pltpu.VMEM((MOE_ITEM_ROWS, D_MODEL), F32),
            pltpu.SemaphoreType.DMA((MOE_ITEM_TILES,)),
            pltpu.SemaphoreType.DMA(()),
        ],
    )
    return pl.pallas_call(
        _moe_body,
        grid_spec=grid_spec,
        out_shape=jax.ShapeDtypeStruct((n_rows, D_MODEL), F32),
        compiler_params=_params(("arbitrary", "arbitrary")),
        name="moe_ffn",
    )(item_e, item_row0, item_nt, n_used, xs, w_gate_up, w_gate_up,
      b_gate_up.reshape(N_EXPERTS, 1, 2 * D_FF), b_gate_up.reshape(N_EXPERTS, 1, 2 * D_FF),
      w_down, b_down.reshape(N_EXPERTS, 1, D_MODEL))


COMB_TM = 128


def _combine_body(dest_ref, dnext_ref, y_hbm, gate_ref, h_ref, g_ref, o_ref, stage, sem):
    t = pl.program_id(0)
    n_steps = pl.num_programs(0)
    slot = t % 2

    def pull(idx_ref, r, k, buf):
        return pltpu.make_async_copy(
            y_hbm.at[pl.ds(idx_ref[0, r * TOP_K + k], 1), :],
            stage.at[buf, pl.ds(k * COMB_TM + r, 1), :], sem.at[buf])

    def issue_all(idx_ref, buf):
        def issue(r, carry):
            for k in range(TOP_K):
                pull(idx_ref, r, k, buf).start(priority=k % 2)
            return carry

        lax.fori_loop(0, COMB_TM, issue, 0)

    @pl.when(t == 0)
    def _():
        issue_all(dest_ref, 0)

    @pl.when(t + 1 < n_steps)
    def _():
        issue_all(dnext_ref, 1 - slot)

    pltpu.make_async_copy(y_hbm.at[pl.ds(0, TOP_K * COMB_TM), :], stage.at[slot],
                          sem.at[slot]).wait()

    gates = gate_ref[...]
    ff = gates[:, 0:1] * stage[slot, 0:COMB_TM, :]
    for k in range(1, TOP_K):
        ff = ff + gates[:, k:k + 1] * stage[slot, k * COMB_TM:(k + 1) * COMB_TM, :]
    h = h_ref[...] + ff
    inv = lax.rsqrt(jnp.mean(h * h, axis=-1, keepdims=True) + NORM_EPS)
    o_ref[...] = h * inv * g_ref[...]


def _combine(dest4, y_rows, gates, h1, g):
    t_real, d = h1.shape
    n_steps = t_real // COMB_TM
    dest_blocks = dest4.reshape(n_steps, 1, COMB_TM * TOP_K)
    smem_block = (None, 1, COMB_TM * TOP_K)
    return pl.pallas_call(
        _combine_body,
        grid=(n_steps,),
        in_specs=[
            pl.BlockSpec(smem_block, lambda i: (i, 0, 0), memory_space=pltpu.SMEM),
            pl.BlockSpec(smem_block, lambda i: (jnp.minimum(i + 1, n_steps - 1), 0, 0),
                         memory_space=pltpu.SMEM),
            pl.BlockSpec(memory_space=pl.ANY),
            pl.BlockSpec((COMB_TM, LANES), lambda i: (i, 0)),
            pl.BlockSpec((COMB_TM, d), lambda i: (i, 0)),
            pl.BlockSpec((1, d), lambda i: (0, 0)),
        ],
        out_specs=pl.BlockSpec((COMB_TM, d), lambda i: (i, 0)),
        out_shape=jax.ShapeDtypeStruct((t_real, d), F32),
        scratch_shapes=[
            pltpu.VMEM((2, TOP_K * COMB_TM, d), F32),
            pltpu.SemaphoreType.DMA((2,)),
        ],
        compiler_params=_params(("arbitrary",)),
        name="moe_combine",
    )(dest_blocks, dest_blocks, y_rows, gates, h1, g)


def kernel(x, meta_tokens, norm_mix_g, w_in, b_fgate, b_glu, conv_w, conv_b, gn_g, gn_b, w_out,
           norm_ffn_g, router_w, router_b, w_gate_up, b_gate_up, w_down, b_down, norm_final_g):
    assert x.shape[0] == 1 and norm_mix_g.shape[0] == 1
    xt = x[0]
    t_real = xt.shape[0]
    fcol = 3 * ATT_WIDTH
    w_all = w_in[0].astype(BF16)
    w_glu = w_all[:, fcol + N_HEADS:]
    w_f = jnp.pad(w_all[:, fcol:fcol + N_HEADS], ((0, 0), (0, LANES - N_HEADS)))
    g_mix = norm_mix_g[0].reshape(1, D_MODEL)

    proj, f_real = _inproj(xt, g_mix, w_all, w_glu, w_f, tm=1024, tn=1024)
    proj_meta, f_meta = _inproj(meta_tokens, g_mix, w_all, w_glu, w_f, tm=N_META, tn=512)

    b_f = jnp.pad(b_fgate[0], (0, LANES - N_HEADS)).reshape(1, LANES)
    qa, ka, kam = _fgate_bias(f_meta, f_real, b_f)
    v_meta = proj_meta[:, 2 * ATT_WIDTH:3 * ATT_WIDTH]
    vm_t = v_meta.reshape(N_META, N_HEADS, HEAD_DIM).transpose(1, 2, 0)
    att = _attention(proj, proj_meta, vm_t, qa, ka, kam, tq=512, tk=256)
    conv = _conv_mixer(proj, proj_meta, b_glu[0].reshape(1, -1), conv_w[0],
                       conv_b[0].reshape(1, -1), gn_g[0].reshape(1, -1), gn_b[0].reshape(1, -1),
                       tm=512)

    rw = jnp.pad(router_w[0], ((0, 0), (0, LANES - N_EXPERTS)))
    rw_hi = rw.astype(BF16)
    rw_lo = (rw - rw_hi.astype(F32)).astype(BF16)
    rb = jnp.pad(router_b[0], (0, LANES - N_EXPERTS), constant_values=NEG_BIG).reshape(1, LANES)
    h1, hn_rows, top_i, gates = _outproj_router(
        att, conv, xt, w_out[0].astype(BF16), norm_ffn_g[0].reshape(1, D_MODEL),
        rw_hi, rw_lo, rb, tm=256)

    n_assign = t_real * TOP_K
    n_tiles = (n_assign + N_EXPERTS * (MOE_TM - 1)) // MOE_TM + 1
    n_items = N_EXPERTS + (n_tiles * MOE_TM) // MOE_ITEM_ROWS
    dest, plan = _routing_plan(top_i)
    counts, starts, tiles_e = (plan[r, :N_EXPERTS] for r in range(3))
    n_used = jnp.sum(tiles_e).astype(jnp.int32).reshape(1)
    item_e, item_row0, item_nt = _item_tables(counts, starts, tiles_e, n_items)
    dest4 = dest[:, :TOP_K]

    xs = _dispatch(starts, tiles_e, n_used, dest4, hn_rows, n_tiles)
    y_rows = _moe_ffn(item_e, item_row0, item_nt, n_used, xs, w_gate_up[0], b_gate_up[0],
                      w_down[0], b_down[0], n_items)
    out = _combine(dest4, y_rows, gates, h1, norm_final_g.reshape(1, D_MODEL))
    return out[None]
```

```python
import functools

import jax
import jax.numpy as jnp
import numpy as np
from jax import lax
from jax.experimental import pallas as pl
from jax.experimental.pallas import tpu as pltpu

D_MODEL = 2048
N_META = 16
HEAD_DIM = 128
ATT_WIDTH = 1024
N_HEADS = ATT_WIDTH // HEAD_DIM
CONV_CH = 1024
N_GROUPS = 8
GROUP_CH = CONV_CH // N_GROUPS
CONV_K = 31
N_EXPERTS = 32
TOP_K = 4
D_FF = 2048
SWIGLU_LIMIT = 7.0
SWIGLU_ALPHA = 1.702
NORM_EPS = 1e-5
NEG_BIG = -1e30

LANES = 128
SUBLANES = 8
V7X_VMEM_BYTES = 64 * 1024 * 1024
VMEM_LIMIT = V7X_VMEM_BYTES * 15 // 16

F32 = jnp.float32
BF16 = jnp.bfloat16


def _params(sem, vmem=VMEM_LIMIT):
    return pltpu.CompilerParams(dimension_semantics=sem, vmem_limit_bytes=vmem)


def _inproj_body(x_ref, g_ref, wa_ref, wc_ref, wf_ref, o_ref, f_ref, hn_ref, *, n_att):
    j = pl.program_id(1)

    @pl.when(j == 0)
    def _():
        x = x_ref[...]
        inv = lax.rsqrt(jnp.mean(x * x, axis=-1, keepdims=True) + NORM_EPS)
        hn = (x * inv * g_ref[...]).astype(BF16)
        hn_ref[...] = hn
        f_ref[...] = jnp.dot(hn, wf_ref[...], preferred_element_type=F32)

    @pl.when(j < n_att)
    def _():
        o_ref[...] = jnp.dot(hn_ref[...], wa_ref[...],
                             preferred_element_type=F32).astype(o_ref.dtype)

    @pl.when(j >= n_att)
    def _():
        o_ref[...] = jnp.dot(hn_ref[...], wc_ref[...],
                             preferred_element_type=F32).astype(o_ref.dtype)


def _inproj(x, g, w_all, w_glu, w_f, tm, tn):
    m, d = x.shape
    n_att = 3 * ATT_WIDTH // tn
    n = 3 * ATT_WIDTH + w_glu.shape[1]
    return pl.pallas_call(
        functools.partial(_inproj_body, n_att=n_att),
        grid=(m // tm, n // tn),
        in_specs=[
            pl.BlockSpec((tm, d), lambda i, j: (i, 0)),
            pl.BlockSpec((1, d), lambda i, j: (0, 0)),
            pl.BlockSpec((d, tn), lambda i, j: (0, jnp.minimum(j, n_att - 1))),
            pl.BlockSpec((d, tn), lambda i, j: (0, jnp.maximum(j - n_att, 0))),
            pl.BlockSpec((d, LANES), lambda i, j: (0, 0)),
        ],
        out_specs=[
            pl.BlockSpec((tm, tn), lambda i, j: (i, j)),
            pl.BlockSpec((tm, LANES), lambda i, j: (i, 0)),
        ],
        out_shape=[
            jax.ShapeDtypeStruct((m, n), BF16),
            jax.ShapeDtypeStruct((m, LANES), F32),
        ],
        scratch_shapes=[pltpu.VMEM((tm, d), BF16)],
        compiler_params=_params(("arbitrary", "arbitrary")),
        name="inproj",
    )(x, g, w_all, w_glu, w_f)


ATT_SCALE = 1.0 / float(np.sqrt(HEAD_DIM))
EXP2_MULT = ATT_SCALE * float(np.log2(np.e))
N_SPLIT = 3
FGATE_TB = 512


def _log_sigmoid(x):
    return jnp.minimum(x, 0.0) - jnp.log1p(jnp.exp(-jnp.abs(x)))


def _prefix_rows(lf):
    n = lf.shape[0]
    row = lax.broadcasted_iota(jnp.int32, (n, n), 0)
    col = lax.broadcasted_iota(jnp.int32, (n, n), 1)
    tri = (col <= row).astype(F32)
    return jnp.dot(tri, lf, preferred_element_type=F32, precision=lax.Precision.HIGHEST)


def _bias_columns(c, head):
    lane = lax.broadcasted_iota(jnp.int32, c.shape, 1)
    x = jnp.sum(jnp.where(lane == head, c, 0.0), axis=-1, keepdims=True) * float(np.sqrt(HEAD_DIM))
    hi = x.astype(BF16).astype(F32)
    r1 = x - hi
    mid = r1.astype(BF16).astype(F32)
    lo = r1 - mid
    part = lane % N_SPLIT
    parts = jnp.where(part == 0, hi, jnp.where(part == 1, mid, lo))
    qa = jnp.where(lane < N_SPLIT, parts, jnp.where(lane < 2 * N_SPLIT, 1.0, 0.0))
    ka = jnp.where(lane < N_SPLIT, 1.0, jnp.where(lane < 2 * N_SPLIT, -parts, 0.0))
    return qa.astype(BF16), ka.astype(BF16)


def _fgate_body(fm_ref, f_ref, b_ref, qa_ref, ka_ref, kam_ref, carry_ref):
    @pl.when(pl.program_id(0) == 0)
    def _():
        cm = _prefix_rows(_log_sigmoid(fm_ref[...] + b_ref[...]))
        carry_ref[...] = cm[N_META - 1:N_META, :]
        for head in range(N_HEADS):
            kam_ref[head] = _bias_columns(cm, head)[1]

    c = _prefix_rows(_log_sigmoid(f_ref[...] + b_ref[...])) + carry_ref[...]
    carry_ref[...] = c[FGATE_TB - 1:FGATE_TB, :]
    for head in range(N_HEADS):
        qa, ka = _bias_columns(c, head)
        qa_ref[head] = qa
        ka_ref[head] = ka


def _fgate_bias(f_meta, f_real, b_f):
    t_real = f_real.shape[0]
    return pl.pallas_call(
        _fgate_body,
        grid=(t_real // FGATE_TB,),
        in_specs=[
            pl.BlockSpec((N_META, LANES), lambda i: (0, 0)),
            pl.BlockSpec((FGATE_TB, LANES), lambda i: (i, 0)),
            pl.BlockSpec((1, LANES), lambda i: (0, 0)),
        ],
        out_specs=[
            pl.BlockSpec((N_HEADS, FGATE_TB, LANES), lambda i: (0, i, 0)),
            pl.BlockSpec((N_HEADS, FGATE_TB, LANES), lambda i: (0, i, 0)),
            pl.BlockSpec((N_HEADS, N_META, LANES), lambda i: (0, 0, 0)),
        ],
        out_shape=[
            jax.ShapeDtypeStruct((N_HEADS, t_real, LANES), BF16),
            jax.ShapeDtypeStruct((N_HEADS, t_real, LANES), BF16),
            jax.ShapeDtypeStruct((N_HEADS, N_META, LANES), BF16),
        ],
        scratch_shapes=[pltpu.VMEM((1, LANES), F32)],
        compiler_params=_params(("arbitrary",)),
        name="fgate_bias",
    )(f_meta, f_real, b_f)


ATT_PACK_ROWS = 16
ATT_HEADS_PER_STEP = 2


def _attn_body(q_ref, qa_ref, k_ref, ka_ref, v_ref, km_ref, kam_ref, vmt_ref, o_ref,
               vt_ref, s_a, s_b, p_ref, acc_ref, *, tq):
    i = pl.program_id(1)
    heads, n_blocks, _, tk = vt_ref.shape
    assert tq == 2 * tk
    contract_last = (((1,), (1,)), ((), ()))

    def cols(hh):
        return slice(hh * HEAD_DIM, (hh + 1) * HEAD_DIM)

    @pl.when(i == 0)
    def _():
        for hh in range(heads):
            for blk in range(n_blocks):
                vt_ref[hh, blk] = v_ref[blk * tk:(blk + 1) * tk, cols(hh)].T

    q = [jnp.concatenate([q_ref[:, cols(hh)], qa_ref[hh]], axis=1) for hh in range(heads)]

    def scores_t(hh, k, ka):
        return lax.dot_general(jnp.concatenate([k, ka], axis=1), q[hh], contract_last,
                               preferred_element_type=F32)

    def block_scores(hh, j):
        rows = pl.ds(pl.multiple_of(j * tk, tk), tk)
        return scores_t(hh, k_ref[rows, cols(hh)], ka_ref[hh, rows, :])

    stats = []
    for hh in range(heads):
        s = scores_t(hh, km_ref[:, cols(hh)], kam_ref[hh])
        m = jnp.max(s, axis=0, keepdims=True)
        p = jnp.exp2((s - m) * EXP2_MULT)
        stats += [m, jnp.sum(p, axis=0, keepdims=True)]
        acc_ref[hh] = jnp.dot(vmt_ref[hh], p.astype(BF16), preferred_element_type=F32)

    def absorb(hh, s_ref, j, m_prev, l_prev):
        top = s_ref[hh, 0:SUBLANES, :]
        for r in range(SUBLANES, tk, SUBLANES):
            top = jnp.maximum(top, s_ref[hh, r:r + SUBLANES, :])
        m_new = jnp.maximum(m_prev, jnp.max(top, axis=0, keepdims=True))
        alpha = jnp.exp2((m_prev - m_new) * EXP2_MULT)
        part = jnp.zeros((SUBLANES, tq), F32)
        for r in range(0, tk, ATT_PACK_ROWS):
            p = jnp.exp2((s_ref[hh, r:r + ATT_PACK_ROWS, :] - m_new) * EXP2_MULT)
            part = part + p[0:SUBLANES, :] + p[SUBLANES:, :]
            p_ref[hh, r:r + ATT_PACK_ROWS, :] = p.astype(BF16)
        l_new = alpha * l_prev + jnp.sum(part, axis=0, keepdims=True)
        acc_ref[hh] = alpha * acc_ref[hh] + jnp.dot(vt_ref[hh, j], p_ref[hh],
                                                    preferred_element_type=F32)
        return [m_new, l_new]

    def absorb_all(s_ref, j, st):
        out = []
        for hh in range(heads):
            out += absorb(hh, s_ref, j, st[2 * hh], st[2 * hh + 1])
        return out

    def pair(pi, st):
        for hh in range(heads):
            s_b[hh] = block_scores(hh, 2 * pi + 1)
        st = absorb_all(s_a, 2 * pi, st)
        for hh in range(heads):
            s_a[hh] = block_scores(hh, 2 * pi + 2)
        return tuple(absorb_all(s_b, 2 * pi + 1, st))

    for hh in range(heads):
        s_a[hh] = block_scores(hh, 0)
    stats = lax.fori_loop(0, i, pair, tuple(stats))

    key = lax.broadcasted_iota(jnp.int32, (tk, tq), 0)
    qry = lax.broadcasted_iota(jnp.int32, (tk, tq), 1)
    for hh in range(heads):
        s_b[hh] = jnp.where(key + tk <= qry, block_scores(hh, 2 * i + 1), NEG_BIG)
        s_a[hh] = jnp.where(key <= qry, s_a[hh], NEG_BIG)
    stats = absorb_all(s_b, 2 * i + 1, absorb_all(s_a, 2 * i, stats))
    for hh in range(heads):
        o_ref[:, cols(hh)] = (acc_ref[hh] / stats[2 * hh + 1]).T.astype(o_ref.dtype)


def _attention(proj, proj_meta, vm_t, qa, ka, kam, tq, tk):
    t_real = proj.shape[0]
    hs = ATT_HEADS_PER_STEP
    width = hs * HEAD_DIM
    kcol = ATT_WIDTH // width
    vcol = 2 * ATT_WIDTH // width
    return pl.pallas_call(
        functools.partial(_attn_body, tq=tq),
        grid=(N_HEADS // hs, t_real // tq),
        in_specs=[
            pl.BlockSpec((tq, width), lambda g, i: (i, g)),
            pl.BlockSpec((hs, tq, LANES), lambda g, i: (g, i, 0)),
            pl.BlockSpec((t_real, width), lambda g, i: (0, kcol + g)),
            pl.BlockSpec((hs, t_real, LANES), lambda g, i: (g, 0, 0)),
            pl.BlockSpec((t_real, width), lambda g, i: (0, vcol + g)),
            pl.BlockSpec((N_META, width), lambda g, i: (0, kcol + g)),
            pl.BlockSpec((hs, N_META, LANES), lambda g, i: (g, 0, 0)),
            pl.BlockSpec((hs, HEAD_DIM, N_META), lambda g, i: (g, 0, 0)),
        ],
        out_specs=pl.BlockSpec((tq, width), lambda g, i: (i, g)),
        out_shape=jax.ShapeDtypeStruct((t_real, ATT_WIDTH), BF16),
        scratch_shapes=[
            pltpu.VMEM((hs, t_real // tk, HEAD_DIM, tk), BF16),
            pltpu.VMEM((hs, tk, tq), F32),
            pltpu.VMEM((hs, tk, tq), F32),
            pltpu.VMEM((hs, tk, tq), BF16),
            pltpu.VMEM((hs, HEAD_DIM, tq), F32),
        ],
        compiler_params=_params(("arbitrary", "arbitrary")),
        name="fox_attention",
    )(proj, qa, proj, ka, proj, proj_meta, kam, vm_t)


CONV_HIST = 32
CONV_ROWS = 128


def _glu(a_ref, g_ref, bglu_ref):
    a = a_ref[...].astype(F32) + bglu_ref[:, :CONV_CH]
    g = g_ref[...].astype(F32) + bglu_ref[:, CONV_CH:]
    return a * jax.nn.sigmoid(g)


def _conv_body(a_ref, g_ref, am_ref, gm_ref, bglu_ref, w_ref, cb_ref, gng_ref, gnb_ref,
               o_ref, ubuf, *, tm):
    i = pl.program_id(0)

    @pl.when(i == 0)
    def _():
        ubuf[0:CONV_HIST - N_META, :] = jnp.zeros((CONV_HIST - N_META, CONV_CH), F32)
        ubuf[CONV_HIST - N_META:CONV_HIST, :] = _glu(am_ref, gm_ref, bglu_ref)

    @pl.when(i > 0)
    def _():
        ubuf[0:CONV_HIST, :] = ubuf[tm:tm + CONV_HIST, :]

    ubuf[CONV_HIST:CONV_HIST + tm, :] = _glu(a_ref, g_ref, bglu_ref)

    first = CONV_HIST - (CONV_K - 1)
    for grp in range(N_GROUPS):
        lanes = slice(grp * GROUP_CH, (grp + 1) * GROUP_CH)
        for rc in range(tm // CONV_ROWS):
            base = rc * CONV_ROWS
            acc = jnp.zeros((CONV_ROWS, GROUP_CH), F32)
            for phase in range(SUBLANES):
                slab = CONV_ROWS + (SUBLANES if phase else 0)
                part = jnp.zeros((slab, GROUP_CH), F32)
                for tap in range(CONV_K):
                    if (first + tap) % SUBLANES == phase:
                        row0 = base + (first + tap) // SUBLANES * SUBLANES
                        part = part + ubuf[row0:row0 + slab, lanes] * w_ref[tap:tap + 1, lanes]
                acc = acc + part[phase:phase + CONV_ROWS, :]
            acc = acc + cb_ref[:, lanes]
            mu = jnp.mean(acc, axis=-1, keepdims=True)
            dlt = acc - mu
            var = jnp.mean(dlt * dlt, axis=-1, keepdims=True)
            y = dlt * lax.rsqrt(var + NORM_EPS) * gng_ref[:, lanes] + gnb_ref[:, lanes]
            o_ref[base:base + CONV_ROWS, lanes] = (y * jax.nn.sigmoid(y)).astype(o_ref.dtype)


def _conv_mixer(proj, proj_meta, b_glu, conv_w, conv_b, gn_g, gn_b, tm):
    t_real = proj.shape[0]
    acol = 3 * ATT_WIDTH // CONV_CH
    gcol = acol + 1
    const = lambda i: (0, 0)
    return pl.pallas_call(
        functools.partial(_conv_body, tm=tm),
        grid=(t_real // tm,),
        in_specs=[
            pl.BlockSpec((tm, CONV_CH), lambda i: (i, acol)),
            pl.BlockSpec((tm, CONV_CH), lambda i: (i, gcol)),
            pl.BlockSpec((N_META, CONV_CH), lambda i: (0, acol)),
            pl.BlockSpec((N_META, CONV_CH), lambda i: (0, gcol)),
            pl.BlockSpec((1, 2 * CONV_CH), const),
            pl.BlockSpec((CONV_K, CONV_CH), const),
            pl.BlockSpec((1, CONV_CH), const),
            pl.BlockSpec((1, CONV_CH), const),
            pl.BlockSpec((1, CONV_CH), const),
        ],
        out_specs=pl.BlockSpec((tm, CONV_CH), lambda i: (i, 0)),
        out_shape=jax.ShapeDtypeStruct((t_real, CONV_CH), BF16),
        scratch_shapes=[pltpu.VMEM((tm + CONV_HIST, CONV_CH), F32)],
        compiler_params=_params(("arbitrary",)),
        name="conv_mixer",
    )(proj, proj, proj_meta, proj_meta, b_glu, conv_w, conv_b, gn_g, gn_b)


def _outproj_body(att_ref, conv_ref, x_ref, w_ref, g_ref, rwh_ref, rwl_ref, rb_ref,
                  h_ref, hn_ref, topi_ref, gate_ref):
    mix = jnp.dot(att_ref[...], w_ref[0:ATT_WIDTH, :], preferred_element_type=F32)
    mix = mix + jnp.dot(conv_ref[...], w_ref[ATT_WIDTH:, :], preferred_element_type=F32)
    h = x_ref[...] + mix
    h_ref[...] = h
    inv = lax.rsqrt(jnp.mean(h * h, axis=-1, keepdims=True) + NORM_EPS)
    hn = h * inv * g_ref[...]
    hn_ref[...] = hn
    hn_hi = hn.astype(BF16)

    hn_lo = (hn - hn_hi.astype(F32)).astype(BF16)
    logits = jnp.dot(hn_hi, rwh_ref[...], preferred_element_type=F32)
    logits = logits + jnp.dot(hn_hi, rwl_ref[...], preferred_element_type=F32)
    logits = logits + jnp.dot(hn_lo, rwh_ref[...], preferred_element_type=F32) + rb_ref[...]
    lane = lax.broadcasted_iota(jnp.int32, logits.shape, 1)
    vals = logits
    tops = []
    idxs = []
    for _ in range(TOP_K):
        mx = jnp.max(vals, axis=-1, keepdims=True)
        ix = jnp.min(jnp.where(vals == mx, lane, LANES), axis=-1, keepdims=True)
        tops.append(mx)
        idxs.append(ix)
        vals = jnp.where(lane == ix, -jnp.inf, vals)
    exps = [jnp.exp(v - tops[0]) for v in tops]
    denom = exps[0] + exps[1] + exps[2] + exps[3]
    topi = jnp.full(logits.shape, -1, jnp.int32)
    gate = jnp.zeros(logits.shape, F32)
    for k in range(TOP_K):
        topi = jnp.where(lane == k, idxs[k], topi)
        gate = jnp.where(lane == k, exps[k] / denom, gate)
    topi_ref[...] = topi
    gate_ref[...] = gate


def _outproj_router(att, conv, x, w_out, g, rw_hi, rw_lo, router_b, tm):
    t_real, d = x.shape
    const = lambda i: (0, 0)
    return pl.pallas_call(
        _outproj_body,
        grid=(t_real // tm,),
        in_specs=[
            pl.BlockSpec((tm, ATT_WIDTH), lambda i: (i, 0)),
            pl.BlockSpec((tm, CONV_CH), lambda i: (i, 0)),
            pl.BlockSpec((tm, d), lambda i: (i, 0)),
            pl.BlockSpec((d, d), const),
            pl.BlockSpec((1, d), const),
            pl.BlockSpec((d, LANES), const),
            pl.BlockSpec((d, LANES), const),
            pl.BlockSpec((1, LANES), const),
        ],
        out_specs=[
            pl.BlockSpec((tm, d), lambda i: (i, 0)),
            pl.BlockSpec((tm, d), lambda i: (i, 0)),
            pl.BlockSpec((tm, LANES), lambda i: (i, 0)),
            pl.BlockSpec((tm, LANES), lambda i: (i, 0)),
        ],
        out_shape=[
            jax.ShapeDtypeStruct((t_real, d), F32),
            jax.ShapeDtypeStruct((t_real, d), F32),
            jax.ShapeDtypeStruct((t_real, LANES), jnp.int32),
            jax.ShapeDtypeStruct((t_real, LANES), F32),
        ],
        compiler_params=_params(("arbitrary",)),
        name="outproj_router",
    )(att, conv, x, w_out, g, rw_hi, rw_lo, router_b)


MOE_TM = 256
PLAN_TB = 512
PLAN_ROWS = SUBLANES


def _plan_body(topi_ref, dest_ref, meta_ref, cnt_ref, carry_ref, start_ref):
    phase = pl.program_id(0)
    j = pl.program_id(1)
    topi = topi_ref[...]
    lane = lax.broadcasted_iota(jnp.int32, topi.shape, 1)
    picks = [topi[:, k:k + 1] for k in range(TOP_K)]
    onehot = jnp.zeros(topi.shape, F32)
    for pick in picks:
        onehot = onehot + (lane == pick).astype(F32)
    colsum = jnp.sum(onehot, axis=0, keepdims=True)

    @pl.when((phase == 0) & (j == 0))
    def _():
        cnt_ref[...] = jnp.zeros(cnt_ref.shape, F32)

    @pl.when(phase == 0)
    def _():
        cnt_ref[...] += colsum

    @pl.when((phase == 1) & (j == 0))
    def _():
        cnt = jnp.broadcast_to(cnt_ref[...], (SUBLANES, LANES))
        tiles = jnp.floor((cnt + (MOE_TM - 1)) * (1.0 / MOE_TM))
        padded = tiles * MOE_TM
        row = lax.broadcasted_iota(jnp.int32, (LANES, LANES), 0)
        col = lax.broadcasted_iota(jnp.int32, (LANES, LANES), 1)
        upper = (row <= col).astype(F32)
        pad_end = jnp.dot(padded, upper, preferred_element_type=F32,
                          precision=lax.Precision.HIGHEST)
        start = pad_end - padded
        start_ref[...] = start[0:1, :]
        carry_ref[...] = jnp.zeros(carry_ref.shape, F32)
        sub = lax.broadcasted_iota(jnp.int32, (PLAN_ROWS, LANES), 0)
        meta_ref[...] = jnp.where(sub == 0, cnt, jnp.where(sub == 1, start, tiles)).astype(jnp.int32)

    @pl.when(phase == 1)
    def _():
        row = lax.broadcasted_iota(jnp.int32, (PLAN_TB, PLAN_TB), 0)
        col = lax.broadcasted_iota(jnp.int32, (PLAN_TB, PLAN_TB), 1)
        earlier = (col < row).astype(BF16)
        before = jnp.dot(earlier, onehot.astype(BF16), preferred_element_type=F32)
        slot = before + carry_ref[...] + start_ref[...]
        dest = jnp.zeros(topi.shape, jnp.int32)
        for k, pick in enumerate(picks):
            d_k = jnp.sum(jnp.where(lane == pick, slot, 0.0), axis=-1, keepdims=True)
            dest = jnp.where(lane == k, d_k.astype(jnp.int32), dest)
        dest_ref[...] = dest
        carry_ref[...] += colsum


def _routing_plan(top_i):
    t_real = top_i.shape[0]
    return pl.pallas_call(
        _plan_body,
        grid=(2, t_real // PLAN_TB),
        in_specs=[pl.BlockSpec((PLAN_TB, LANES), lambda p, j: (j, 0))],
        out_specs=[
            pl.BlockSpec((PLAN_TB, LANES), lambda p, j: (j * p, 0)),
            pl.BlockSpec((PLAN_ROWS, LANES), lambda p, j: (0, 0)),
        ],
        out_shape=[
            jax.ShapeDtypeStruct((t_real, LANES), jnp.int32),
            jax.ShapeDtypeStruct((PLAN_ROWS, LANES), jnp.int32),
        ],
        scratch_shapes=[pltpu.VMEM((1, LANES), F32)] * 3,
        compiler_params=_params(("arbitrary", "arbitrary")),
        name="routing_plan",
    )(top_i)


MOE_ITEM_TILES = 5
MOE_ITEM_ROWS = MOE_ITEM_TILES * MOE_TM


def _item_tables(counts, starts, tiles_e, n_items):
    items_e = (tiles_e + MOE_ITEM_TILES - 1) // MOE_ITEM_TILES
    item_end = jnp.cumsum(items_e)
    item_start = item_end - items_e
    wid = jnp.arange(n_items, dtype=jnp.int32)
    e_of = jnp.minimum(jnp.sum((item_end[None, :] <= wid[:, None]).astype(jnp.int32), axis=1),
                       N_EXPERTS - 1)
    k_of = wid - item_start[e_of]
    live = wid < item_end[-1]
    last_e = jnp.max(jnp.where(counts > 0, jnp.arange(N_EXPERTS, dtype=jnp.int32), 0))
    item_e = jnp.where(live, e_of, last_e).astype(jnp.int32)
    item_nt = jnp.where(live, jnp.minimum(MOE_ITEM_TILES, tiles_e[e_of] - k_of * MOE_ITEM_TILES), 0)
    item_row0 = jnp.where(live, starts[e_of] + k_of * MOE_ITEM_ROWS, 0)
    return item_e, item_row0.astype(jnp.int32), item_nt.astype(jnp.int32)


DISP_TB = 256


def _dispatch_body(start_ref, tiles_ref, nused_ref, dest_ref, src_ref, xs_hbm,
                   rows, ztile, sem, zsem):
    t = pl.program_id(0)
    n_steps = pl.num_programs(0)
    n_tiles = xs_hbm.shape[0] // MOE_TM
    pushes = DISP_TB * TOP_K
    slot = t % 2
    rows[slot] = src_ref[...]

    @pl.when(t == 0)
    def _():
        ztile[...] = jnp.zeros(ztile.shape, ztile.dtype)

        def clear(tix):
            return pltpu.make_async_copy(
                ztile, xs_hbm.at[pl.ds(pl.multiple_of(tix * MOE_TM, MOE_TM), MOE_TM), :], zsem)

        def last_tile(e):
            return start_ref[e] // MOE_TM + tiles_ref[e] - 1

        def start_group(e, carry):
            @pl.when(tiles_ref[e] > 0)
            def _():
                clear(last_tile(e)).start()
            return carry

        def wait_group(e, carry):
            @pl.when(tiles_ref[e] > 0)
            def _():
                clear(last_tile(e)).wait()
            return carry

        lax.fori_loop(0, N_EXPERTS, start_group, 0)
        lax.fori_loop(nused_ref[0], n_tiles, lambda tix, c: (clear(tix).start(), c)[1], 0)
        lax.fori_loop(0, N_EXPERTS, wait_group, 0)
        lax.fori_loop(nused_ref[0], n_tiles, lambda tix, c: (clear(tix).wait(), c)[1], 0)

    def issue(r, carry):
        for k in range(TOP_K):
            pltpu.make_async_copy(
                rows.at[slot, pl.ds(r, 1), :],
                xs_hbm.at[pl.ds(dest_ref[0, r * TOP_K + k], 1), :],
                sem.at[slot]).start(priority=k % 2)
        return carry

    lax.fori_loop(0, DISP_TB, issue, 0)

    def retire_step(buf):
        pltpu.make_async_copy(xs_hbm.at[pl.ds(0, pushes), :], xs_hbm.at[pl.ds(0, pushes), :],
                              sem.at[buf]).wait()

    @pl.when(t > 0)
    def _():
        retire_step(1 - slot)

    @pl.when(t == n_steps - 1)
    def _():
        retire_step(slot)


def _dispatch(starts, tiles_e, n_used, dest4, hn_rows, n_tiles):
    t_real = hn_rows.shape[0]
    n_steps = t_real // DISP_TB
    grid_spec = pltpu.PrefetchScalarGridSpec(
        num_scalar_prefetch=3,
        grid=(n_steps,),
        in_specs=[
            pl.BlockSpec((None, 1, DISP_TB * TOP_K), lambda t, *_: (t, 0, 0),
                         memory_space=pltpu.SMEM),
            pl.BlockSpec((DISP_TB, D_MODEL), lambda t, *_: (t, 0)),
        ],
        out_specs=pl.BlockSpec(memory_space=pl.ANY),
        scratch_shapes=[
            pltpu.VMEM((2, DISP_TB, D_MODEL), F32),
            pltpu.VMEM((MOE_TM, D_MODEL), F32),
            pltpu.SemaphoreType.DMA((2,)),
            pltpu.SemaphoreType.DMA(()),
        ],
    )
    return pl.pallas_call(
        _dispatch_body,
        grid_spec=grid_spec,
        out_shape=jax.ShapeDtypeStruct((n_tiles * MOE_TM, D_MODEL), F32),
        compiler_params=_params(("arbitrary",)),
        name="moe_dispatch",
    )(starts, tiles_e, n_used, dest4.reshape(n_steps, 1, DISP_TB * TOP_K), hn_rows)


MOE_TF = 512
MOE_NF = D_FF // MOE_TF


def _moe_body(e_ref, row0_ref, nt_ref, nused_ref, xs_hbm, wg_ref, wu_ref, bg_ref, bu_ref,
              wd_ref, bd_ref, y_hbm, xq, yacc, sem_x, sem_y):
    w = pl.program_id(0)
    f = pl.program_id(1)
    n_items = pl.num_programs(0)
    nt = nt_ref[w]
    slot = w % 2

    def x_copy(item, r, buf):
        return pltpu.make_async_copy(
            xs_hbm.at[pl.ds(pl.multiple_of(row0_ref[item] + r * MOE_TM, MOE_TM), MOE_TM), :],
            xq.at[buf, pl.ds(pl.multiple_of(r * MOE_TM, MOE_TM), MOE_TM), :], sem_x.at[buf, r])

    def y_copy(r):
        return pltpu.make_async_copy(
            yacc.at[pl.ds(pl.multiple_of(r * MOE_TM, MOE_TM), MOE_TM), :],
            y_hbm.at[pl.ds(pl.multiple_of(row0_ref[w] + r * MOE_TM, MOE_TM), MOE_TM), :], sem_y)

    def for_tiles(item, fn):
        for r in range(MOE_ITEM_TILES):
            @pl.when(r < nt_ref[item])
            def _():
                fn(r)

    @pl.when(f == 0)
    def _():
        @pl.when(w == 0)
        def _():
            for_tiles(0, lambda r: x_copy(0, r, 0).start())

        @pl.when(w + 1 < n_items)
        def _():
            nxt = jnp.minimum(w + 1, n_items - 1)
            for_tiles(nxt, lambda r: x_copy(nxt, r, 1 - slot).start())

    @pl.when(nt > 0)
    def _():
        def tile_rows(r):
            return pl.ds(pl.multiple_of(r * MOE_TM, MOE_TM), MOE_TM)

        def hidden(r):
            @pl.when(f == 0)
            def _():
                x_copy(w, r, slot).wait()

            x = xq[slot, tile_rows(r), :].astype(BF16)
            gate = jnp.dot(x, wg_ref[...].astype(BF16), preferred_element_type=F32) + bg_ref[...]
            up = jnp.dot(x, wu_ref[...].astype(BF16), preferred_element_type=F32) + bu_ref[...]
            gate = jnp.minimum(gate, SWIGLU_LIMIT)
            up = jnp.clip(up, -SWIGLU_LIMIT, SWIGLU_LIMIT)
            return (gate * jax.nn.sigmoid(SWIGLU_ALPHA * gate) * (up + 1.0)).astype(BF16)

        def project(r, act):
            yacc[tile_rows(r), :] += jnp.dot(act, wd_ref[...].astype(BF16),
                                             preferred_element_type=F32)

            @pl.when(f == MOE_NF - 1)
            def _():
                y_copy(r).start()

        def tile(r, act_prev):
            act = hidden(r)
            project(r - 1, act_prev)
            return act

        @pl.when(f == 0)
        def _():
            def seed(r, carry):
                yacc[tile_rows(r), :] = jnp.broadcast_to(bd_ref[...], (MOE_TM, D_MODEL))
                return carry

            lax.fori_loop(0, nt, seed, 0)

        project(nt - 1, lax.fori_loop(1, nt, tile, hidden(0)))

        @pl.when(f == MOE_NF - 1)
        def _():
            for_tiles(w, lambda r: y_copy(r).wait())

    @pl.when((w == n_items - 1) & (f == MOE_NF - 1))
    def _():
        yacc[0:MOE_TM, :] = jnp.zeros((MOE_TM, D_MODEL), F32)

        def zero_tile(tix, carry):
            cp = pltpu.make_async_copy(
                yacc.at[pl.ds(0, MOE_TM), :],
                y_hbm.at[pl.ds(pl.multiple_of(tix * MOE_TM, MOE_TM), MOE_TM), :], sem_y)
            cp.start()
            cp.wait()
            return carry

        lax.fori_loop(nused_ref[0], y_hbm.shape[0] // MOE_TM, zero_tile, 0)


def _moe_ffn(item_e, item_row0, item_nt, n_used, xs, w_gate_up, b_gate_up, w_down, b_down, n_items):
    n_rows = xs.shape[0]

    def fidx(w, f, nt):
        return jnp.where(nt[w] > 0, f, MOE_NF - 1)

    grid_spec = pltpu.PrefetchScalarGridSpec(
        num_scalar_prefetch=4,
        grid=(n_items, MOE_NF),
        in_specs=[
            pl.BlockSpec(memory_space=pl.ANY),
            pl.BlockSpec((None, D_MODEL, MOE_TF),
                         lambda w, f, e, r0, nt, nu: (e[w], 0, fidx(w, f, nt))),
            pl.BlockSpec((None, D_MODEL, MOE_TF),
                         lambda w, f, e, r0, nt, nu: (e[w], 0, MOE_NF + fidx(w, f, nt))),
            pl.BlockSpec((None, 1, MOE_TF), lambda w, f, e, r0, nt, nu: (e[w], 0, fidx(w, f, nt))),
            pl.BlockSpec((None, 1, MOE_TF),
                         lambda w, f, e, r0, nt, nu: (e[w], 0, MOE_NF + fidx(w, f, nt))),
            pl.BlockSpec((None, MOE_TF, D_MODEL),
                         lambda w, f, e, r0, nt, nu: (e[w], fidx(w, f, nt), 0)),
            pl.BlockSpec((None, 1, D_MODEL), lambda w, f, e, r0, nt, nu: (e[w], 0, 0)),
        ],
        out_specs=pl.BlockSpec(memory_space=pl.ANY),
        scratch_shapes=[
            pltpu.VMEM((2, MOE_ITEM_ROWS, D_MODEL), F32),
            pltpu.VMEM((MOE_ITEM_ROWS, D_MODEL), F32),
            pltpu.SemaphoreType.DMA((2, MOE_ITEM_TILES)),
            pltpu.SemaphoreType.DMA(()),
        ],
    )
    return pl.pallas_call(
        _moe_body,
        grid_spec=grid_spec,
        out_shape=jax.ShapeDtypeStruct((n_rows, D_MODEL), F32),
        compiler_params=_params(("arbitrary", "arbitrary")),
        name="moe_ffn",
    )(item_e, item_row0, item_nt, n_used, xs, w_gate_up, w_gate_up,
      b_gate_up.reshape(N_EXPERTS, 1, 2 * D_FF), b_gate_up.reshape(N_EXPERTS, 1, 2 * D_FF),
      w_down, b_down.reshape(N_EXPERTS, 1, D_MODEL))


COMB_TM = 128


def _combine_body(dest_ref, dnext_ref, y_hbm, gate_ref, h_ref, g_ref, o_ref, stage, sem):
    t = pl.program_id(0)
    n_steps = pl.num_programs(0)
    slot = t % 2

    def pull(idx_ref, r, k, buf):
        return pltpu.make_async_copy(
            y_hbm.at[pl.ds(idx_ref[0, r * TOP_K + k], 1), :],
            stage.at[buf, pl.ds(k * COMB_TM + r, 1), :], sem.at[buf])

    def issue_all(idx_ref, buf):
        def issue(r, carry):
            for k in range(TOP_K):
                pull(idx_ref, r, k, buf).start(priority=k % 2)
            return carry

        lax.fori_loop(0, COMB_TM, issue, 0)

    @pl.when(t == 0)
    def _():
        issue_all(dest_ref, 0)

    @pl.when(t + 1 < n_steps)
    def _():
        issue_all(dnext_ref, 1 - slot)

    pltpu.make_async_copy(y_hbm.at[pl.ds(0, TOP_K * COMB_TM), :], stage.at[slot],
                          sem.at[slot]).wait()

    gates = gate_ref[...]
    ff = gates[:, 0:1] * stage[slot, 0:COMB_TM, :]
    for k in range(1, TOP_K):
        ff = ff + gates[:, k:k + 1] * stage[slot, k * COMB_TM:(k + 1) * COMB_TM, :]
    h = h_ref[...] + ff
    inv = lax.rsqrt(jnp.mean(h * h, axis=-1, keepdims=True) + NORM_EPS)
    o_ref[...] = h * inv * g_ref[...]


def _combine(dest4, y_rows, gates, h1, g):
    t_real, d = h1.shape
    n_steps = t_real // COMB_TM
    dest_blocks = dest4.reshape(n_steps, 1, COMB_TM * TOP_K)
    smem_block = (None, 1, COMB_TM * TOP_K)
    return pl.pallas_call(
        _combine_body,
        grid=(n_steps,),
        in_specs=[
            pl.BlockSpec(smem_block, lambda i: (i, 0, 0), memory_space=pltpu.SMEM),
            pl.BlockSpec(smem_block, lambda i: (jnp.minimum(i + 1, n_steps - 1), 0, 0),
                         memory_space=pltpu.SMEM),
            pl.BlockSpec(memory_space=pl.ANY),
            pl.BlockSpec((COMB_TM, LANES), lambda i: (i, 0)),
            pl.BlockSpec((COMB_TM, d), lambda i: (i, 0)),
            pl.BlockSpec((1, d), lambda i: (0, 0)),
        ],
        out_specs=pl.BlockSpec((COMB_TM, d), lambda i: (i, 0)),
        out_shape=jax.ShapeDtypeStruct((t_real, d), F32),
        scratch_shapes=[
            pltpu.VMEM((2, TOP_K * COMB_TM, d), F32),
            pltpu.SemaphoreType.DMA((2,)),
        ],
        compiler_params=_params(("arbitrary",)),
        name="moe_combine",
    )(dest_blocks, dest_blocks, y_rows, gates, h1, g)


def kernel(x, meta_tokens, norm_mix_g, w_in, b_fgate, b_glu, conv_w, conv_b, gn_g, gn_b, w_out,
           norm_ffn_g, router_w, router_b, w_gate_up, b_gate_up, w_down, b_down, norm_final_g):
    assert x.shape[0] == 1 and norm_mix_g.shape[0] == 1
    xt = x[0]
    t_real = xt.shape[0]
    fcol = 3 * ATT_WIDTH
    w_all = w_in[0].astype(BF16)
    w_glu = w_all[:, fcol + N_HEADS:]
    w_f = jnp.pad(w_all[:, fcol:fcol + N_HEADS], ((0, 0), (0, LANES - N_HEADS)))
    g_mix = norm_mix_g[0].reshape(1, D_MODEL)

    proj, f_real = _inproj(xt, g_mix, w_all, w_glu, w_f, tm=1024, tn=1024)
    proj_meta, f_meta = _inproj(meta_tokens, g_mix, w_all, w_glu, w_f, tm=N_META, tn=512)

    b_f = jnp.pad(b_fgate[0], (0, LANES - N_HEADS)).reshape(1, LANES)
    qa, ka, kam = _fgate_bias(f_meta, f_real, b_f)
    v_meta = proj_meta[:, 2 * ATT_WIDTH:3 * ATT_WIDTH]
    vm_t = v_meta.reshape(N_META, N_HEADS, HEAD_DIM).transpose(1, 2, 0)
    att = _attention(proj, proj_meta, vm_t, qa, ka, kam, tq=512, tk=256)
    conv = _conv_mixer(proj, proj_meta, b_glu[0].reshape(1, -1), conv_w[0],
                       conv_b[0].reshape(1, -1), gn_g[0].reshape(1, -1), gn_b[0].reshape(1, -1),
                       tm=512)

    rw = jnp.pad(router_w[0], ((0, 0), (0, LANES - N_EXPERTS)))
    rw_hi = rw.astype(BF16)
    rw_lo = (rw - rw_hi.astype(F32)).astype(BF16)
    rb = jnp.pad(router_b[0], (0, LANES - N_EXPERTS), constant_values=NEG_BIG).reshape(1, LANES)
    h1, hn_rows, top_i, gates = _outproj_router(
        att, conv, xt, w_out[0].astype(BF16), norm_ffn_g[0].reshape(1, D_MODEL),
        rw_hi, rw_lo, rb, tm=256)

    n_assign = t_real * TOP_K
    n_tiles = (n_assign + N_EXPERTS * (MOE_TM - 1)) // MOE_TM + 1
    n_items = N_EXPERTS + (n_tiles * MOE_TM) // MOE_ITEM_ROWS
    dest, plan = _routing_plan(top_i)
    counts, starts, tiles_e = (plan[r, :N_EXPERTS] for r in range(3))
    n_used = jnp.sum(tiles_e).astype(jnp.int32).reshape(1)
    item_e, item_row0, item_nt = _item_tables(counts, starts, tiles_e, n_items)
    dest4 = dest[:, :TOP_K]

    xs = _dispatch(starts, tiles_e, n_used, dest4, hn_rows, n_tiles)
    y_rows = _moe_ffn(item_e, item_row0, item_nt, n_used, xs, w_gate_up[0], b_gate_up[0],
                      w_down[0], b_down[0], n_items)
    out = _combine(dest4, y_rows, gates, h1, norm_final_g.reshape(1, D_MODEL))
    return out[None]
```
